```python
import math
import jax, jax.numpy as jnp
from jax import lax
import numpy as np

D_MODEL = 1024
BATCH = 2
SEQ = 8192
DEPTH = 1
DEC_BATCH = 32
DEC_SEQ = 4
PAST_LEN = 8192
PAGE_SIZE = 128

N_HEADS = 8
HEAD_DIM = D_MODEL // (2 * N_HEADS)
V_DIM = 2 * HEAD_DIM
QK_WIDTH = N_HEADS * 2 * HEAD_DIM
ATT_WIDTH = N_HEADS * V_DIM
Q_BLOCK = 128
CHUNK = 128
GROUP_WIDTH = 128
N_GROUPS = D_MODEL // GROUP_WIDTH
GMLP_WIDTH = N_GROUPS * GROUP_WIDTH
D_FF = 2816
Q_OFF = 0
K_OFF = Q_OFF + QK_WIDTH
V_OFF = K_OFF + QK_WIDTH
U_OFF = V_OFF + ATT_WIDTH
VG_OFF = U_OFF + GMLP_WIDTH
GA_OFF = VG_OFF + GMLP_WIDTH
GM_OFF = GA_OFF + D_MODEL
D_IN = GM_OFF + D_MODEL
ALPHA = (2 * DEPTH) ** 0.25
BETA = (8 * DEPTH) ** -0.25
LN_EPS = 1e-5
NEG_INF = -1e30

kernel_name = "diffattn_gmlp_macaron_deepnorm_step"


def alibi_slopes():
    return 2.0 ** (-8.0 * jnp.arange(1, N_HEADS + 1, dtype=jnp.float32) / N_HEADS)


def layer_norm(x, g, b):
    xf = x.astype(jnp.float32)
    mu = jnp.mean(xf, axis=-1, keepdims=True)
    var = jnp.mean(jnp.square(xf - mu), axis=-1, keepdims=True)
    return ((xf - mu) * lax.rsqrt(var + LN_EPS) * g + b).astype(x.dtype)


def swiglu(x, wg, wu, wd):
    return (jax.nn.silu(x @ wg) * (x @ wu)) @ wd


def diff_lambda(lq1, lk1, lq2, lk2, lam_init):
    f = jnp.float32
    return (jnp.exp(jnp.sum(lq1.astype(f) * lk1.astype(f)))
            - jnp.exp(jnp.sum(lq2.astype(f) * lk2.astype(f))) + lam_init)


def diff_attn_prompt(q, k, v, lam):
    B, S = q.shape[0], q.shape[1]
    nb = S // Q_BLOCK
    scale = HEAD_DIM ** -0.5
    slopes = alibi_slopes()
    qb = q.reshape(B, nb, Q_BLOCK, N_HEADS, 2, HEAD_DIM).transpose(1, 0, 2, 3, 4, 5)
    kpos = jnp.arange(S)

    def block(args):
        qi, start = args
        s = jnp.einsum('bqhjd,bkhjd->bjhqk', qi, k).astype(jnp.float32) * scale
        dist = (start + jnp.arange(Q_BLOCK))[:, None] - kpos[None, :]
        bias = jnp.where(dist[None] >= 0, -slopes[:, None, None] * dist[None].astype(jnp.float32), NEG_INF)
        p = jax.nn.softmax(s + bias, axis=-1)
        a = p[:, 0] - lam * p[:, 1]
        return jnp.einsum('bhqk,bkhe->bqhe', a.astype(v.dtype), v)

    out = lax.map(block, (qb, jnp.arange(nb) * Q_BLOCK))
    return out.transpose(1, 0, 2, 3, 4).reshape(B, S, N_HEADS, V_DIM)


def diff_attn_sample(q, k, v, k_past, v_past, lam):
    DB, T = q.shape[0], q.shape[1]
    P = k_past.shape[1]
    scale = HEAD_DIM ** -0.5
    slopes = alibi_slopes()
    kp = k_past.reshape(DB, P, N_HEADS, 2, HEAD_DIM)
    s_past = jnp.einsum('bqhjd,bkhjd->bjhqk', q, kp).astype(jnp.float32) * scale
    s_new = jnp.einsum('bqhjd,bkhjd->bjhqk', q, k).astype(jnp.float32) * scale
    dist_past = (P + jnp.arange(T))[:, None] - jnp.arange(P)[None, :]
    bias_past = -slopes[:, None, None] * dist_past[None].astype(jnp.float32)
    dist_new = jnp.arange(T)[:, None] - jnp.arange(T)[None, :]
    bias_new = jnp.where(dist_new[None] >= 0, -slopes[:, None, None] * dist_new[None].astype(jnp.float32), NEG_INF)
    p = jax.nn.softmax(jnp.concatenate([s_past + bias_past, s_new + bias_new], axis=-1), axis=-1)
    a = (p[:, 0] - lam * p[:, 1]).astype(v.dtype)
    return (jnp.einsum('bhqk,bkhe->bqhe', a[..., :P], v_past)
            + jnp.einsum('bhqk,bkhe->bqhe', a[..., P:], v))


def head_rmsnorm(o, g, lam_init):
    of = o.astype(jnp.float32)
    of = of * lax.rsqrt(jnp.mean(jnp.square(of), axis=-1, keepdims=True) + LN_EPS)
    return (of * g * (1.0 - lam_init)).astype(o.dtype)


def spatial_gating(u, vg, ln_g, ln_b, w_s, b_s):
    B, T = vg.shape[0], vg.shape[1]
    vn = layer_norm(vg, ln_g, ln_b)
    Tp = -(-T // CHUNK) * CHUNK
    vp = jnp.pad(vn, ((0, 0), (0, Tp - T), (0, 0)))
    vc = vp.reshape(B, Tp // CHUNK, CHUNK, N_GROUPS, GROUP_WIDTH)
    w = w_s * jnp.tril(jnp.ones((CHUNK, CHUNK), dtype=w_s.dtype))
    f = jnp.einsum('gts,bnsgc->bntgc', w, vc) + b_s.T[None, None, :, :, None]
    f = f.reshape(B, Tp, GMLP_WIDTH)[:, :T]
    return u * f, vn


def token_mix(h, attend, w_in, lq1, lk1, lq2, lk2, subln_g, lam_init,
              gmlp_ln_g, gmlp_ln_b, w_spatial, b_spatial, w_out):
    B, T = h.shape[0], h.shape[1]
    z = h @ w_in
    q, k, v, u, vg, ga, gm = jnp.split(z, [K_OFF, V_OFF, U_OFF, VG_OFF, GA_OFF, GM_OFF], axis=-1)
    q = q.reshape(B, T, N_HEADS, 2, HEAD_DIM)
    k_rows = k.reshape(B, T, N_HEADS, V_DIM)
    v_rows = v.reshape(B, T, N_HEADS, V_DIM)
    lam = diff_lambda(lq1, lk1, lq2, lk2, lam_init)
    o = attend(q, k_rows.reshape(B, T, N_HEADS, 2, HEAD_DIM), v_rows, lam)
    o = head_rmsnorm(o, subln_g, lam_init).reshape(B, T, ATT_WIDTH)
    s, vn = spatial_gating(jax.nn.gelu(u, approximate=False), jax.nn.gelu(vg, approximate=False),
                           gmlp_ln_g, gmlp_ln_b, w_spatial, b_spatial)
    mix = (jax.nn.sigmoid(ga) * o + jax.nn.sigmoid(gm) * s) @ w_out
    return mix, k_rows, v_rows, vn


def setup_inputs(seed: int = 0) -> dict:
    key = jax.random.key(seed)
    ks = jax.random.split(key, 40)
    n_pages = PAST_LEN // PAGE_SIZE
    n_used = DEC_BATCH * n_pages
    n_pool = n_used + max(1, n_used // 4)
    nrm = lambda i, shape, s: jax.random.normal(ks[i], shape, jnp.float32) * s
    w_in_scale = jnp.ones((D_IN,), jnp.float32).at[V_OFF:U_OFF].set(BETA)
    return {
        "x_prompt": nrm(0, (BATCH, SEQ, D_MODEL), 1.0),
        "x_sample": nrm(1, (DEC_BATCH, DEC_SEQ, D_MODEL), 1.0),
        "cache_k": nrm(2, (DEPTH, n_pool, PAGE_SIZE, N_HEADS, V_DIM), 1.0),
        "cache_v": nrm(3, (DEPTH, n_pool, PAGE_SIZE, N_HEADS, V_DIM), 1.0),
        "page_table": jax.random.permutation(ks[4], n_pool)[:n_used].reshape(DEC_BATCH, n_pages).astype(jnp.int32),
        "ln1_g": 1.0 + nrm(5, (DEPTH, D_MODEL), 0.02),
        "ln1_b": nrm(6, (DEPTH, D_MODEL), 0.02),
        "ffn1_wg": nrm(7, (DEPTH, D_MODEL, D_FF), D_MODEL ** -0.5),
        "ffn1_wu": nrm(8, (DEPTH, D_MODEL, D_FF), D_MODEL ** -0.5),
        "ffn1_wd": nrm(9, (DEPTH, D_FF, D_MODEL), BETA * D_FF ** -0.5),
        "w_in": nrm(10, (DEPTH, D_MODEL, D_IN), D_MODEL ** -0.5) * w_in_scale,
        "lam_q1": nrm(11, (DEPTH, HEAD_DIM), 0.1),
        "lam_k1": nrm(12, (DEPTH, HEAD_DIM), 0.1),
        "lam_q2": nrm(13, (DEPTH, HEAD_DIM), 0.1),
        "lam_k2": nrm(14, (DEPTH, HEAD_DIM), 0.1),
        "subln_g": 1.0 + nrm(15, (DEPTH, V_DIM), 0.02),
        "gmlp_ln_g": 1.0 + nrm(16, (DEPTH, GMLP_WIDTH), 0.02),
        "gmlp_ln_b": nrm(17, (DEPTH, GMLP_WIDTH), 0.02),
        "w_spatial": nrm(18, (DEPTH, N_GROUPS, CHUNK, CHUNK), CHUNK ** -0.5),
        "b_spatial": 1.0 + nrm(19, (DEPTH, N_GROUPS, CHUNK), 0.01),
        "w_out": nrm(20, (DEPTH, D_MODEL, D_MODEL), BETA * D_MODEL ** -0.5),
        "ln2_g": 1.0 + nrm(21, (DEPTH, D_MODEL), 0.02),
        "ln2_b": nrm(22, (DEPTH, D_MODEL), 0.02),
        "ffn2_wg": nrm(23, (DEPTH, D_MODEL, D_FF), D_MODEL ** -0.5),
        "ffn2_wu": nrm(24, (DEPTH, D_MODEL, D_FF), D_MODEL ** -0.5),
        "ffn2_wd": nrm(25, (DEPTH, D_FF, D_MODEL), BETA * D_FF ** -0.5),
        "ln3_g": 1.0 + nrm(26, (DEPTH, D_MODEL), 0.02),
        "ln3_b": nrm(27, (DEPTH, D_MODEL), 0.02),
    }


def reference(x_prompt, x_sample, cache_k, cache_v, page_table,
              ln1_g, ln1_b, ffn1_wg, ffn1_wu, ffn1_wd,
              w_in, lam_q1, lam_k1, lam_q2, lam_k2, subln_g,
              gmlp_ln_g, gmlp_ln_b, w_spatial, b_spatial, w_out,
              ln2_g, ln2_b, ffn2_wg, ffn2_wu, ffn2_wd, ln3_g, ln3_b):
    DB = x_sample.shape[0]
    hp, hs = x_prompt, x_sample
    kp_l, vp_l, ks_l, vs_l, gv_l = [], [], [], [], []
    for l in range(DEPTH):
        lam_init = 0.8 - 0.6 * math.exp(-0.3 * l)
        k_past = cache_k[l, page_table].reshape(DB, -1, N_HEADS, V_DIM)
        v_past = cache_v[l, page_table].reshape(DB, -1, N_HEADS, V_DIM)
        attend_s = lambda q, k, v, lam: diff_attn_sample(q, k, v, k_past, v_past, lam)
        mix_w = (w_in[l], lam_q1[l], lam_k1[l], lam_q2[l], lam_k2[l], subln_g[l], lam_init,
                 gmlp_ln_g[l], gmlp_ln_b[l], w_spatial[l], b_spatial[l], w_out[l])
        outs = []
        for h, attend in ((hp, diff_attn_prompt), (hs, attend_s)):
            h = layer_norm(ALPHA * h + 0.5 * swiglu(h, ffn1_wg[l], ffn1_wu[l], ffn1_wd[l]), ln1_g[l], ln1_b[l])
            mix, k_rows, v_rows, vn = token_mix(h, attend, *mix_w)
            h = layer_norm(ALPHA * h + mix, ln2_g[l], ln2_b[l])
            h = layer_norm(ALPHA * h + 0.5 * swiglu(h, ffn2_wg[l], ffn2_wu[l], ffn2_wd[l]), ln3_g[l], ln3_b[l])
            outs.append((h, k_rows, v_rows, vn))
        (hp, kpr, vpr, _), (hs, ksr, vsr, gvs) = outs
        kp_l.append(kpr); vp_l.append(vpr); ks_l.append(ksr); vs_l.append(vsr); gv_l.append(gvs)
    return (hp, hs, jnp.stack(kp_l), jnp.stack(vp_l), jnp.stack(ks_l), jnp.stack(vs_l), jnp.stack(gv_l))
```

```python
import functools
import math

import jax
import jax.numpy as jnp
from jax import lax
from jax.experimental import pallas as pl
from jax.experimental.pallas import tpu as pltpu

D_MODEL = 1024
DEPTH = 1
PAGE_SIZE = 128
N_HEADS = 8
HEAD_DIM = 64
V_DIM = 2 * HEAD_DIM
CHUNK = 128
GROUP_WIDTH = 128
N_GROUPS = D_MODEL // GROUP_WIDTH
D_FF = 2816
ALPHA = (2 * DEPTH) ** 0.25
LN_EPS = 1e-5
NEG_INF = -1e30
QK_SCALE = HEAD_DIM ** -0.5

VMEM_LIMIT_BYTES = 56 * 1024 * 1024
ATTN_TILE = 256
PAGES_PER_STEP = 8

F32 = jnp.float32
BF16 = jnp.bfloat16


def _slopes():
    return [2.0 ** (-8.0 * (h + 1) / N_HEADS) for h in range(N_HEADS)]


def _layer_norm(x, g, b):
    mu = jnp.mean(x, axis=-1, keepdims=True)
    xc = x - mu
    var = jnp.mean(xc * xc, axis=-1, keepdims=True)
    return xc * lax.rsqrt(var + LN_EPS) * g + b


def _gelu(x):
    return 0.5 * x * (1.0 + lax.erf(x * math.sqrt(0.5)))


def _swiglu(xb, wg_ref, wu_ref, wd_ref):
    gate = jnp.dot(xb, wg_ref[...], preferred_element_type=F32)
    up = jnp.dot(xb, wu_ref[...], preferred_element_type=F32)
    act = (gate * jax.nn.sigmoid(gate)) * up
    return jnp.dot(act.astype(BF16), wd_ref[...], preferred_element_type=F32)


def _diff_lambda(lam_ref, lam_init):
    lp = lam_ref[...]
    a = jnp.sum(lp[0:1] * lp[1:2], axis=-1, keepdims=True)
    b = jnp.sum(lp[2:3] * lp[3:4], axis=-1, keepdims=True)
    return jnp.exp(a) - jnp.exp(b) + lam_init


def _const_spec(shape):
    nd = len(shape)
    return pl.BlockSpec(shape, lambda *_: (0,) * nd, pipeline_mode=pl.Buffered(1))


def _params(n_axes):
    return pltpu.CompilerParams(
        dimension_semantics=("arbitrary",) * n_axes, vmem_limit_bytes=VMEM_LIMIT_BYTES)


def _ffn_ln_kernel(x_ref, wg_ref, wu_ref, wd_ref, g_ref, b_ref, o_ref):
    x = x_ref[...]
    y = _swiglu(x.astype(BF16), wg_ref, wu_ref, wd_ref)
    o_ref[...] = _layer_norm(ALPHA * x + 0.5 * y, g_ref[...], b_ref[...])


def _ffn_ln(x, wg, wu, wd, g, b, *, tm):
    rows = x.shape[0]
    row_spec = pl.BlockSpec((tm, D_MODEL), lambda i: (i, 0))
    return pl.pallas_call(
        _ffn_ln_kernel,
        grid=(rows // tm,),
        in_specs=[row_spec, _const_spec(wg.shape), _const_spec(wu.shape), _const_spec(wd.shape),
                  _const_spec(g.shape), _const_spec(b.shape)],
        out_specs=row_spec,
        out_shape=jax.ShapeDtypeStruct((rows, D_MODEL), F32),
        compiler_params=_params(1),
        name="ffn_ln",
    )(x, wg, wu, wd, g, b)


def _proj_kernel(h_ref, w_ref, lg_ref, lb_ref, *out_refs, attn_layouts):
    hb = h_ref[...].astype(BF16)

    def section(i):
        return jnp.dot(hb, w_ref[:, i * D_MODEL:(i + 1) * D_MODEL], preferred_element_type=F32)

    q = section(0) * QK_SCALE
    k = section(1)
    v = section(2)
    if attn_layouts:
        k_ref, v_ref, qt_ref, kb_ref, vt_ref, gu_ref, vn_ref, sga_ref, sgm_ref = out_refs
        qt_ref[0, 0] = q.T.astype(BF16)
        kb_ref[...] = k.astype(BF16)
        vt_ref[0, 0] = v.T.astype(BF16)
    else:
        q_ref, k_ref, v_ref, gu_ref, vn_ref, sga_ref, sgm_ref = out_refs
        q_ref[...] = q
    k_ref[...] = k
    v_ref[...] = v
    gu_ref[...] = _gelu(section(3))
    vn_ref[...] = _layer_norm(_gelu(section(4)), lg_ref[...], lb_ref[...])
    sga_ref[...] = jax.nn.sigmoid(section(5))
    sgm_ref[...] = jax.nn.sigmoid(section(6))


def _proj(h, w_in, ln_g, ln_b, *, tm, batch=None):
    rows = h.shape[0]
    n_tiles = rows // tm
    row_spec = pl.BlockSpec((tm, D_MODEL), lambda i: (i, 0))
    row_f32 = jax.ShapeDtypeStruct((rows, D_MODEL), F32)
    attn_layouts = batch is not None
    if attn_layouts:
        per_b = n_tiles // batch
        t_spec = pl.BlockSpec((1, 1, D_MODEL, tm), lambda i: (i // per_b, i % per_b, 0, 0))
        t_shape = jax.ShapeDtypeStruct((batch, per_b, D_MODEL, tm), BF16)
        out_specs = [row_spec, row_spec, t_spec, row_spec, t_spec] + [row_spec] * 4
        out_shape = [row_f32, row_f32, t_shape, jax.ShapeDtypeStruct((rows, D_MODEL), BF16), t_shape] + [row_f32] * 4
    else:
        out_specs = [row_spec] * 7
        out_shape = [row_f32] * 7
    return pl.pallas_call(
        functools.partial(_proj_kernel, attn_layouts=attn_layouts),
        grid=(n_tiles,),
        in_specs=[row_spec, _const_spec(w_in.shape), _const_spec(ln_g.shape), _const_spec(ln_b.shape)],
        out_specs=out_specs,
        out_shape=out_shape,
        compiler_params=_params(1),
        name="proj",
    )(h, w_in, ln_g, ln_b)


def _attn_prompt_kernel(qt_ref, k_ref, vt_ref, bias_ref, slope_ref, lam_ref, g_ref, o_ref,
                        w1_ref, w2_ref, acc1_ref, acc2_ref, *, lam_init):
    t = ATTN_TILE
    qi = pl.program_id(2)
    qt = qt_ref[0, 0]
    zeros = jnp.zeros((HEAD_DIM, t), BF16)
    w1_ref[0:HEAD_DIM, :] = qt[0:HEAD_DIM]
    w1_ref[HEAD_DIM:, :] = zeros
    w2_ref[0:HEAD_DIM, :] = zeros
    w2_ref[HEAD_DIM:, :] = qt[HEAD_DIM:]
    acc1_ref[...] = jnp.zeros_like(acc1_ref)
    acc2_ref[...] = jnp.zeros_like(acc2_ref)

    def branch(w_ref, acc_ref, kblk, vblk, bias, d, m, l):
        s = jnp.dot(kblk, w_ref[...], preferred_element_type=F32) + bias
        m_new = jnp.maximum(m, jnp.max(s, axis=0, keepdims=True) + d)
        a = jnp.exp(m - m_new)
        p = jnp.exp(s - (m_new - d))
        l_new = a * l + jnp.sum(p, axis=0, keepdims=True)
        acc_ref[...] = a * acc_ref[...] + jnp.dot(vblk, p.astype(BF16), preferred_element_type=F32)
        return m_new, l_new

    def body(j, carry):
        m1, l1, m2, l2 = carry
        kblk = k_ref[0, j]
        vblk = vt_ref[0, j]
        bias = bias_ref[0, (j == qi).astype(jnp.int32)]
        d = slope_ref[0] * jnp.full((1, t), j - qi, jnp.int32).astype(F32)
        m1, l1 = branch(w1_ref, acc1_ref, kblk, vblk, bias, d, m1, l1)
        m2, l2 = branch(w2_ref, acc2_ref, kblk, vblk, bias, d, m2, l2)
        return m1, l1, m2, l2

    m0 = jnp.full((1, t), NEG_INF, F32)
    l0 = jnp.zeros((1, t), F32)
    _, l1, _, l2 = lax.fori_loop(0, qi + 1, body, (m0, l0, m0, l0))

    lam = _diff_lambda(lam_ref, lam_init)
    ot = acc1_ref[...] / l1 - lam * (acc2_ref[...] / l2)
    ot = ot * lax.rsqrt(jnp.mean(ot * ot, axis=0, keepdims=True) + LN_EPS)
    o_ref[...] = ot.T * g_ref[...] * (1.0 - lam_init)


def _attn_prompt(qt, kb, vt, lam_p, subln_g, *, lam_init):
    batch, nblk, _, t = qt.shape
    assert t == ATTN_TILE
    kb = kb.reshape(batch, nblk, t, D_MODEL)
    r = jnp.arange(t, dtype=F32)
    rel = r[None, :] - r[:, None]
    slopes = jnp.asarray(_slopes(), F32)
    off_diag = -slopes[:, None, None] * rel[None]
    diag = jnp.where(rel[None] >= 0, off_diag, NEG_INF)
    bias = jnp.stack([off_diag, diag], axis=1)
    slope_t = jnp.broadcast_to((slopes * t)[:, None, None], (N_HEADS, 1, t))
    return pl.pallas_call(
        functools.partial(_attn_prompt_kernel, lam_init=lam_init),
        grid=(batch, N_HEADS, nblk),
        in_specs=[
            pl.BlockSpec((1, 1, V_DIM, t), lambda b, h, i: (b, i, h, 0)),
            pl.BlockSpec((1, nblk, t, V_DIM), lambda b, h, i: (b, 0, 0, h)),
            pl.BlockSpec((1, nblk, V_DIM, t), lambda b, h, i: (b, 0, h, 0)),
            pl.BlockSpec((1, 2, t, t), lambda b, h, i: (h, 0, 0, 0)),
            pl.BlockSpec((1, 1, t), lambda b, h, i: (h, 0, 0)),
            pl.BlockSpec(lam_p.shape, lambda b, h, i: (0, 0)),
            pl.BlockSpec(subln_g.shape, lambda b, h, i: (0, 0)),
        ],
        out_specs=pl.BlockSpec((t, V_DIM), lambda b, h, i: (b * nblk + i, h)),
        out_shape=jax.ShapeDtypeStruct((batch * nblk * t, D_MODEL), F32),
        scratch_shapes=[pltpu.VMEM((V_DIM, t), BF16), pltpu.VMEM((V_DIM, t), BF16),
                        pltpu.VMEM((V_DIM, t), F32), pltpu.VMEM((V_DIM, t), F32)],
        compiler_params=_params(3),
        name="attn_prompt",
    )(qt, kb, vt, bias, slope_t, lam_p, subln_g)


def _attn_decode_kernel(pt_ref, q_ref, kn_ref, vn_ref, *rest, lam_init, past_len, n_tok):
    g_pages = PAGES_PER_STEP
    k_refs = rest[:g_pages]
    v_refs = rest[g_pages:2 * g_pages]
    bias_ref, biasn_ref, slope_ref, lam_ref, g_ref, o_ref, qm_ref, m_ref, l_ref, acc_ref = rest[2 * g_pages:]
    del pt_ref
    step = pl.program_id(1)
    rows = n_tok * N_HEADS

    @pl.when(step == 0)
    def _():
        q = q_ref[0]
        lane = lax.broadcasted_iota(jnp.int32, q.shape, 1)
        qm_ref[0:rows, :] = jnp.where(lane < HEAD_DIM, q, 0.0).astype(BF16)
        qm_ref[rows:, :] = jnp.where(lane >= HEAD_DIM, q, 0.0).astype(BF16)
        m_ref[...] = jnp.full_like(m_ref, NEG_INF)
        l_ref[...] = jnp.zeros_like(l_ref)
        acc_ref[...] = jnp.zeros_like(acc_ref)

    def update(kp, vp, bias, d):
        s = lax.dot_general(qm_ref[...], kp, (((1,), (1,)), ((), ())), preferred_element_type=F32) + bias
        m = m_ref[...]
        m_new = jnp.maximum(m, jnp.max(s, axis=1, keepdims=True) + d)
        a = jnp.exp(m - m_new)
        p = jnp.exp(s - (m_new - d))
        l_ref[...] = a * l_ref[...] + jnp.sum(p, axis=1, keepdims=True)
        acc_ref[...] = a * acc_ref[...] + jnp.dot(p.astype(BF16), vp, preferred_element_type=F32)
        m_ref[...] = m_new

    slope = slope_ref[...]
    for g in range(g_pages):
        page = step * g_pages + g
        d = slope * jnp.full((2 * rows, 1), page * PAGE_SIZE - past_len, jnp.int32).astype(F32)
        update(k_refs[g][0].astype(BF16), v_refs[g][0].astype(BF16), bias_ref[...], d)

    @pl.when(step == pl.num_programs(1) - 1)
    def _():
        update(kn_ref[0].astype(BF16), vn_ref[0].astype(BF16), biasn_ref[...], jnp.zeros((2 * rows, 1), F32))
        lam = _diff_lambda(lam_ref, lam_init)
        on = acc_ref[...] / l_ref[...]
        o = on[0:rows] - lam * on[rows:]
        o = o * lax.rsqrt(jnp.mean(o * o, axis=-1, keepdims=True) + LN_EPS)
        o_ref[0] = o * g_ref[...] * (1.0 - lam_init)


def _attn_decode(q, k_new, v_new, cache_k, cache_v, page_table, lam_p, subln_g, *, lam_init):
    db, rows, _ = q.shape
    n_tok = rows // N_HEADS
    n_pages = page_table.shape[1]
    past_len = n_pages * PAGE_SIZE
    g_pages = PAGES_PER_STEP
    assert n_pages % g_pages == 0
    slopes = jnp.asarray(_slopes(), F32)
    row_h = jnp.tile(jnp.arange(N_HEADS), 2 * n_tok)
    row_t = jnp.tile(jnp.repeat(jnp.arange(n_tok), N_HEADS), 2)
    row_slope = slopes[row_h]
    col_pos = jnp.repeat(jnp.arange(PAGE_SIZE), N_HEADS)
    col_h = jnp.tile(jnp.arange(N_HEADS), PAGE_SIZE)
    bias = jnp.where(row_h[:, None] == col_h[None, :],
                     row_slope[:, None] * (col_pos[None, :] - row_t[:, None]).astype(F32), NEG_INF)
    ncol_t = jnp.repeat(jnp.arange(n_tok), N_HEADS)
    ncol_h = jnp.tile(jnp.arange(N_HEADS), n_tok)
    dist = row_t[:, None] - ncol_t[None, :]
    bias_new = jnp.where((row_h[:, None] == ncol_h[None, :]) & (dist >= 0),
                         -row_slope[:, None] * dist.astype(F32), NEG_INF)
    slope_col = row_slope[:, None]

    tok_spec = pl.BlockSpec((1, rows, V_DIM), lambda b, s, pt: (b, 0, 0))

    def page_spec(g):
        return pl.BlockSpec((1, PAGE_SIZE * N_HEADS, V_DIM), lambda b, s, pt: (pt[b, s * g_pages + g], 0, 0))

    def const2(shape):
        return pl.BlockSpec(shape, lambda b, s, pt: (0, 0))

    grid_spec = pltpu.PrefetchScalarGridSpec(
        num_scalar_prefetch=1,
        grid=(db, n_pages // g_pages),
        in_specs=[tok_spec, tok_spec, tok_spec]
        + [page_spec(g) for g in range(g_pages)] * 2
        + [const2(bias.shape), const2(bias_new.shape), const2(slope_col.shape),
           const2(lam_p.shape), const2(subln_g.shape)],
        out_specs=tok_spec,
        scratch_shapes=[pltpu.VMEM((2 * rows, V_DIM), BF16), pltpu.VMEM((2 * rows, 1), F32),
                        pltpu.VMEM((2 * rows, 1), F32), pltpu.VMEM((2 * rows, V_DIM), F32)],
    )
    return pl.pallas_call(
        functools.partial(_attn_decode_kernel, lam_init=lam_init, past_len=past_len, n_tok=n_tok),
        grid_spec=grid_spec,
        out_shape=jax.ShapeDtypeStruct((db, rows, V_DIM), F32),
        compiler_params=_params(2),
        name="attn_decode",
    )(page_table, q, k_new, v_new, *([cache_k] * g_pages), *([cache_v] * g_pages),
      bias, bias_new, slope_col, lam_p, subln_g)


def _mix_ffn_kernel(h_ref, o_ref, sga_ref, sgm_ref, gu_ref, vn_ref, wsp_ref, bsp_ref, wo_ref,
                    g2_ref, b2_ref, wg_ref, wu_ref, wd_ref, g3_ref, b3_ref, y_ref, s_ref, *, period):
    tm = h_ref.shape[0]
    r = lax.broadcasted_iota(jnp.int32, (CHUNK, CHUNK), 0)
    c = lax.broadcasted_iota(jnp.int32, (CHUNK, CHUNK), 1)
    keep = (r // period == c // period) & (c <= r)
    for grp in range(N_GROUPS):
        w = jnp.where(keep, wsp_ref[grp], 0.0).astype(BF16)
        cols = slice(grp * GROUP_WIDTH, (grp + 1) * GROUP_WIDTH)
        for ch in range(tm // CHUNK):
            rws = slice(ch * CHUNK, (ch + 1) * CHUNK)
            f = jnp.dot(w, vn_ref[rws, cols].astype(BF16), preferred_element_type=F32) + bsp_ref[grp]
            s_ref[rws, cols] = gu_ref[rws, cols] * f
    mix_in = sga_ref[...] * o_ref[...] + sgm_ref[...] * s_ref[...]
    mix = jnp.dot(mix_in.astype(BF16), wo_ref[...], preferred_element_type=F32)
    h2 = _layer_norm(ALPHA * h_ref[...] + mix, g2_ref[...], b2_ref[...])
    y = _swiglu(h2.astype(BF16), wg_ref, wu_ref, wd_ref)
    y_ref[...] = _layer_norm(ALPHA * h2 + 0.5 * y, g3_ref[...], b3_ref[...])


def _mix_ffn(h, o, sga, sgm, gu, vn, wsp, bsp, wo, g2, b2, wg, wu, wd, g3, b3, *, tm, period):
    rows = h.shape[0]
    row_spec = pl.BlockSpec((tm, D_MODEL), lambda i: (i, 0))
    consts = (wsp, bsp, wo, g2, b2, wg, wu, wd, g3, b3)
    return pl.pallas_call(
        functools.partial(_mix_ffn_kernel, period=period),
        grid=(rows // tm,),
        in_specs=[row_spec] * 6 + [_const_spec(a.shape) for a in consts],
        out_specs=row_spec,
        out_shape=jax.ShapeDtypeStruct((rows, D_MODEL), F32),
        scratch_shapes=[pltpu.VMEM((tm, D_MODEL), F32)],
        compiler_params=_params(1),
        name="mix_ffn",
    )(h, o, sga, sgm, gu, vn, *consts)


def _spatial_params(w_spatial, b_spatial, period):
    reps = CHUNK // period
    w = jnp.tile(w_spatial[:, :period, :period], (1, reps, reps))
    b = jnp.tile(b_spatial[:, :period], (1, reps))
    return w, jnp.broadcast_to(b[:, :, None], (N_GROUPS, CHUNK, GROUP_WIDTH))


def kernel(x_prompt, x_sample, cache_k, cache_v, page_table, ln1_g, ln1_b, ffn1_wg, ffn1_wu, ffn1_wd, w_in, lam_q1, lam_k1, lam_q2, lam_k2, subln_g, gmlp_ln_g, gmlp_ln_b, w_spatial, b_spatial, w_out, ln2_g, ln2_b, ffn2_wg, ffn2_wu, ffn2_wd, ln3_g, ln3_b):
    batch, seq, _ = x_prompt.shape
    db, n_tok, _ = x_sample.shape
    rows_p, rows_s = batch * seq, db * n_tok
    assert rows_s == CHUNK and CHUNK % n_tok == 0 and seq % ATTN_TILE == 0
    hp = x_prompt.reshape(rows_p, D_MODEL)
    hs = x_sample.reshape(rows_s, D_MODEL)
    row = lambda a: a.reshape(1, -1)
    tm = ATTN_TILE
    kp_l, vp_l, ks_l, vs_l, gv_l = [], [], [], [], []
    for l in range(DEPTH):
        lam_init = 0.8 - 0.6 * math.exp(-0.3 * l)
        wg1, wu1, wd1 = ffn1_wg[l].astype(BF16), ffn1_wu[l].astype(BF16), ffn1_wd[l].astype(BF16)
        wg2, wu2, wd2 = ffn2_wg[l].astype(BF16), ffn2_wu[l].astype(BF16), ffn2_wd[l].astype(BF16)
        w_in_b, w_out_b = w_in[l].astype(BF16), w_out[l].astype(BF16)
        lam_p = jnp.stack([lam_q1[l], lam_k1[l], lam_q2[l], lam_k2[l]])
        sub_g = row(subln_g[l])
        ffn1 = (wg1, wu1, wd1, row(ln1_g[l]), row(ln1_b[l]))
        tail = (w_out_b, row(ln2_g[l]), row(ln2_b[l]), wg2, wu2, wd2, row(ln3_g[l]), row(ln3_b[l]))
        gln = (row(gmlp_ln_g[l]), row(gmlp_ln_b[l]))

        h1 = _ffn_ln(hp, *ffn1, tm=tm)
        k_p, v_p, qt, kb, vt, gu, vn, sga, sgm = _proj(h1, w_in_b, *gln, tm=tm, batch=batch)
        o = _attn_prompt(qt, kb, vt, lam_p, sub_g, lam_init=lam_init)
        wsp, bsp = _spatial_params(w_spatial[l], b_spatial[l], CHUNK)
        hp = _mix_ffn(h1, o, sga, sgm, gu, vn, wsp, bsp, *tail, tm=tm, period=CHUNK)

        h1 = _ffn_ln(hs, *ffn1, tm=rows_s)
        q_s, k_s, v_s, gu, vn_s, sga, sgm = _proj(h1, w_in_b, *gln, tm=rows_s)
        tok = lambda a: a.reshape(db, n_tok * N_HEADS, V_DIM)
        o = _attn_decode(tok(q_s), tok(k_s), tok(v_s),
                         cache_k[l].reshape(-1, PAGE_SIZE * N_HEADS, V_DIM),
                         cache_v[l].reshape(-1, PAGE_SIZE * N_HEADS, V_DIM),
                         page_table, lam_p, sub_g, lam_init=lam_init)
        wsp, bsp = _spatial_params(w_spatial[l], b_spatial[l], n_tok)
        hs = _mix_ffn(h1, o.reshape(rows_s, D_MODEL), sga, sgm, gu, vn_s, wsp, bsp, *tail, tm=rows_s, period=n_tok)

        kp_l.append(k_p.reshape(batch, seq, N_HEADS, V_DIM))
        vp_l.append(v_p.reshape(batch, seq, N_HEADS, V_DIM))
        ks_l.append(k_s.reshape(db, n_tok, N_HEADS, V_DIM))
        vs_l.append(v_s.reshape(db, n_tok, N_HEADS, V_DIM))
        gv_l.append(vn_s.reshape(db, n_tok, D_MODEL))
    return (hp.reshape(batch, seq, D_MODEL), hs.reshape(db, n_tok, D_MODEL),
            jnp.stack(kp_l), jnp.stack(vp_l), jnp.stack(ks_l), jnp.stack(vs_l), jnp.stack(gv_l))
```

```python
import functools
import math

import jax
import jax.numpy as jnp
from jax import lax
from jax.experimental import pallas as pl
from jax.experimental.pallas import tpu as pltpu

D_MODEL = 1024
DEPTH = 1
PAGE_SIZE = 128
N_HEADS = 8
HEAD_DIM = 64
V_DIM = 2 * HEAD_DIM
CHUNK = 128
GROUP_WIDTH = 128
N_GROUPS = D_MODEL // GROUP_WIDTH
D_FF = 2816
ALPHA = (2 * DEPTH) ** 0.25
LN_EPS = 1e-5
NEG_INF = -1e30
QK_SCALE = HEAD_DIM ** -0.5

VMEM_LIMIT_BYTES = 56 * 1024 * 1024
ATTN_TQ = 256
ATTN_TK = 1024
N_POS_COLS = 2
PAGES_PER_STEP = 8

F32 = jnp.float32
BF16 = jnp.bfloat16


def _slopes():
    return [2.0 ** (-8.0 * (h + 1) / N_HEADS) for h in range(N_HEADS)]


def _layer_norm(x, g, b):
    mu = jnp.mean(x, axis=-1, keepdims=True)
    xc = x - mu
    var = jnp.mean(xc * xc, axis=-1, keepdims=True)
    return xc * lax.rsqrt(var + LN_EPS) * g + b


def _gelu(x):
    return 0.5 * x * (1.0 + lax.erf(x * math.sqrt(0.5)))


def _swiglu(xb, wg_ref, wu_ref, wd_ref):
    gate = jnp.dot(xb, wg_ref[...], preferred_element_type=F32)
    up = jnp.dot(xb, wu_ref[...], preferred_element_type=F32)
    act = (gate * jax.nn.sigmoid(gate)) * up
    return jnp.dot(act.astype(BF16), wd_ref[...], preferred_element_type=F32)


def _diff_lambda(lam_ref, lam_init):
    lp = lam_ref[...]
    a = jnp.sum(lp[0:1] * lp[1:2], axis=-1, keepdims=True)
    b = jnp.sum(lp[2:3] * lp[3:4], axis=-1, keepdims=True)
    return jnp.exp(a) - jnp.exp(b) + lam_init


def _const_spec(shape):
    nd = len(shape)
    return pl.BlockSpec(shape, lambda *_: (0,) * nd, pipeline_mode=pl.Buffered(1))


def _params(n_axes):
    return pltpu.CompilerParams(
        dimension_semantics=("arbitrary",) * n_axes, vmem_limit_bytes=VMEM_LIMIT_BYTES)


def _ffn_ln_kernel(x_ref, wg_ref, wu_ref, wd_ref, g_ref, b_ref, o_ref):
    x = x_ref[...]
    y = _swiglu(x.astype(BF16), wg_ref, wu_ref, wd_ref)
    o_ref[...] = _layer_norm(ALPHA * x + 0.5 * y, g_ref[...], b_ref[...])


def _ffn_ln(x, wg, wu, wd, g, b, *, tm):
    rows = x.shape[0]
    row_spec = pl.BlockSpec((tm, D_MODEL), lambda i: (i, 0))
    return pl.pallas_call(
        _ffn_ln_kernel,
        grid=(rows // tm,),
        in_specs=[row_spec, _const_spec(wg.shape), _const_spec(wu.shape), _const_spec(wd.shape),
                  _const_spec(g.shape), _const_spec(b.shape)],
        out_specs=row_spec,
        out_shape=jax.ShapeDtypeStruct((rows, D_MODEL), F32),
        compiler_params=_params(1),
        name="ffn_ln",
    )(x, wg, wu, wd, g, b)


def _proj_kernel(h_ref, w_ref, lg_ref, lb_ref, *out_refs, attn_layouts):
    hb = h_ref[...].astype(BF16)

    def section(i):
        return jnp.dot(hb, w_ref[:, i * D_MODEL:(i + 1) * D_MODEL], preferred_element_type=F32)

    q = section(0) * QK_SCALE
    k = section(1)
    v = section(2)
    if attn_layouts:
        k_ref, v_ref, qt_ref, kb_ref, vt_ref, gu_ref, vn_ref, sga_ref, sgm_ref = out_refs
        qt_ref[0, 0] = q.T.astype(BF16)
        kb_ref[...] = k.astype(BF16)
        vt_ref[0, 0] = v.T.astype(BF16)
    else:
        q_ref, k_ref, v_ref, gu_ref, vn_ref, sga_ref, sgm_ref = out_refs
        q_ref[...] = q
    k_ref[...] = k
    v_ref[...] = v
    gu_ref[...] = _gelu(section(3))
    vn_ref[...] = _layer_norm(_gelu(section(4)), lg_ref[...], lb_ref[...])
    sga_ref[...] = jax.nn.sigmoid(section(5))
    sgm_ref[...] = jax.nn.sigmoid(section(6))


def _proj(h, w_in, ln_g, ln_b, *, tm, batch=None):
    rows = h.shape[0]
    n_tiles = rows // tm
    row_spec = pl.BlockSpec((tm, D_MODEL), lambda i: (i, 0))
    row_f32 = jax.ShapeDtypeStruct((rows, D_MODEL), F32)
    attn_layouts = batch is not None
    if attn_layouts:
        per_b = n_tiles // batch
        t_spec = pl.BlockSpec((1, 1, D_MODEL, tm), lambda i: (i // per_b, i % per_b, 0, 0))
        t_shape = jax.ShapeDtypeStruct((batch, per_b, D_MODEL, tm), BF16)
        out_specs = [row_spec, row_spec, t_spec, row_spec, t_spec] + [row_spec] * 4
        out_shape = [row_f32, row_f32, t_shape, jax.ShapeDtypeStruct((rows, D_MODEL), BF16), t_shape] + [row_f32] * 4
    else:
        out_specs = [row_spec] * 7
        out_shape = [row_f32] * 7
    return pl.pallas_call(
        functools.partial(_proj_kernel, attn_layouts=attn_layouts),
        grid=(n_tiles,),
        in_specs=[row_spec, _const_spec(w_in.shape), _const_spec(ln_g.shape), _const_spec(ln_b.shape)],
        out_specs=out_specs,
        out_shape=out_shape,
        compiler_params=_params(1),
        name="proj",
    )(h, w_in, ln_g, ln_b)


def _attn_prompt_kernel(qt_ref, k_ref, vt_ref, pos_ref, rel_ref, slope_ref, lam_ref, g_ref, o_ref,
                        w_ref, acc_ref, *, lam_init):
    tq, tk = ATTN_TQ, ATTN_TK
    sub = tk // tq
    qi = pl.program_id(2)
    qt = qt_ref[0, 0]
    zeros = jnp.zeros((HEAD_DIM, tq), BF16)
    w_ref[0:HEAD_DIM, 0:tq] = qt[0:HEAD_DIM]
    w_ref[HEAD_DIM:V_DIM, 0:tq] = zeros
    w_ref[0:HEAD_DIM, tq:] = zeros
    w_ref[HEAD_DIM:V_DIM, tq:] = qt[HEAD_DIM:]
    prow = lax.broadcasted_iota(jnp.int32, (V_DIM, 2 * tq), 0)
    w_ref[V_DIM:, :] = jnp.where(prow < N_POS_COLS, 1.0, 0.0).astype(BF16)
    acc_ref[...] = jnp.zeros_like(acc_ref)

    def chunk(j, m, l, masked):
        lhs = jnp.concatenate([k_ref[0, j], pos_ref[0]], axis=1)
        s = jnp.dot(lhs, w_ref[...], preferred_element_type=F32)
        if masked:
            lim = jnp.full((1, 2 * tq), (qi % sub) * tq, jnp.int32).astype(F32)
            s = jnp.where(rel_ref[...] <= lim, s, NEG_INF)
        d = slope_ref[0] * jnp.full((1, 2 * tq), j * tk - qi * tq, jnp.int32).astype(F32)
        m_new = jnp.maximum(m, jnp.max(s, axis=0, keepdims=True) + d)
        a = jnp.exp(m - m_new)
        p = jnp.exp(s - (m_new - d))
        l_new = a * l + jnp.sum(p, axis=0, keepdims=True)
        vt = jnp.concatenate([vt_ref[0, j * sub + u] for u in range(sub)], axis=1)
        acc_ref[...] = a * acc_ref[...] + jnp.dot(vt, p.astype(BF16), preferred_element_type=F32)
        return m_new, l_new

    m0 = jnp.full((1, 2 * tq), NEG_INF, F32)
    l0 = jnp.zeros((1, 2 * tq), F32)
    n_full = qi // sub
    m, l = lax.fori_loop(0, n_full, lambda j, c: chunk(j, c[0], c[1], False), (m0, l0))
    _, l = chunk(n_full, m, l, True)

    lam = _diff_lambda(lam_ref, lam_init)
    on = acc_ref[...] / l
    ot = on[:, 0:tq] - lam * on[:, tq:]
    ot = ot * lax.rsqrt(jnp.mean(ot * ot, axis=0, keepdims=True) + LN_EPS)
    o_ref[...] = ot.T * g_ref[...] * (1.0 - lam_init)


def _attn_prompt(qt, kb, vt, lam_p, subln_g, *, lam_init):
    batch, nblk, _, tq = qt.shape
    tk = ATTN_TK
    assert tq == ATTN_TQ and (nblk * tq) % tk == 0
    nchunk = nblk * tq // tk
    kb = kb.reshape(batch, nchunk, tk, D_MODEL)
    slopes = jnp.asarray(_slopes(), F32)
    c = jnp.arange(tk)
    pos = jnp.zeros((N_HEADS, tk, V_DIM), F32)
    pos = pos.at[:, :, 0].set(slopes[:, None] * ((c // 256) * 256).astype(F32)[None])
    pos = pos.at[:, :, 1].set(slopes[:, None] * (c % 256).astype(F32)[None])
    pos = pos.astype(BF16)
    rel = (c[:, None] - jnp.tile(jnp.arange(tq), 2)[None, :]).astype(F32)
    slope_row = jnp.broadcast_to(slopes[:, None, None], (N_HEADS, 1, 2 * tq))
    return pl.pallas_call(
        functools.partial(_attn_prompt_kernel, lam_init=lam_init),
        grid=(batch, N_HEADS, nblk),
        in_specs=[
            pl.BlockSpec((1, 1, V_DIM, tq), lambda b, h, i: (b, i, h, 0)),
            pl.BlockSpec((1, nchunk, tk, V_DIM), lambda b, h, i: (b, 0, 0, h)),
            pl.BlockSpec((1, nblk, V_DIM, tq), lambda b, h, i: (b, 0, h, 0)),
            pl.BlockSpec((1, tk, V_DIM), lambda b, h, i: (h, 0, 0)),
            pl.BlockSpec(rel.shape, lambda b, h, i: (0, 0), pipeline_mode=pl.Buffered(1)),
            pl.BlockSpec((1, 1, 2 * tq), lambda b, h, i: (h, 0, 0)),
            pl.BlockSpec(lam_p.shape, lambda b, h, i: (0, 0)),
            pl.BlockSpec(subln_g.shape, lambda b, h, i: (0, 0)),
        ],
        out_specs=pl.BlockSpec((tq, V_DIM), lambda b, h, i: (b * nblk + i, h)),
        out_shape=jax.ShapeDtypeStruct((batch * nblk * tq, D_MODEL), F32),
        scratch_shapes=[pltpu.VMEM((2 * V_DIM, 2 * tq), BF16), pltpu.VMEM((V_DIM, 2 * tq), F32)],
        compiler_params=_params(3),
        name="attn_prompt",
    )(qt, kb, vt, pos, rel, slope_row, lam_p, subln_g)


def _attn_decode_kernel(pt_ref, q_ref, kn_ref, vn_ref, *rest, lam_init, past_len, n_tok):
    g_pages = PAGES_PER_STEP
    k_refs = rest[:g_pages]
    v_refs = rest[g_pages:2 * g_pages]
    bias_ref, biasn_ref, slope_ref, lam_ref, g_ref, o_ref, qm_ref, m_ref, l_ref, acc_ref = rest[2 * g_pages:]
    del pt_ref
    step = pl.program_id(1)
    rows = n_tok * N_HEADS

    @pl.when(step == 0)
    def _():
        q = q_ref[0]
        lane = lax.broadcasted_iota(jnp.int32, q.shape, 1)
        qm_ref[0:rows, :] = jnp.where(lane < HEAD_DIM, q, 0.0).astype(BF16)
        qm_ref[rows:, :] = jnp.where(lane >= HEAD_DIM, q, 0.0).astype(BF16)
        m_ref[...] = jnp.full_like(m_ref, NEG_INF)
        l_ref[...] = jnp.zeros_like(l_ref)
        acc_ref[...] = jnp.zeros_like(acc_ref)

    def update(pages):
        qm = qm_ref[...]
        s_list, mx = [], None
        for kp, _, bias, d in pages:
            s = lax.dot_general(qm, kp, (((1,), (1,)), ((), ())), preferred_element_type=F32) + bias
            s_list.append(s)
            blk = jnp.max(s, axis=1, keepdims=True) + d
            mx = blk if mx is None else jnp.maximum(mx, blk)
        m = m_ref[...]
        m_new = jnp.maximum(m, mx)
        a = jnp.exp(m - m_new)
        l = a * l_ref[...]
        acc = a * acc_ref[...]
        for s, (_, vp, _, d) in zip(s_list, pages):
            p = jnp.exp(s - (m_new - d))
            l = l + jnp.sum(p, axis=1, keepdims=True)
            acc = acc + jnp.dot(p.astype(BF16), vp, preferred_element_type=F32)
        l_ref[...] = l
        acc_ref[...] = acc
        m_ref[...] = m_new

    slope = slope_ref[...]
    pages = []
    for g in range(g_pages):
        page = step * g_pages + g
        d = slope * jnp.full((2 * rows, 1), page * PAGE_SIZE - past_len, jnp.int32).astype(F32)
        pages.append((k_refs[g][0].astype(BF16), v_refs[g][0].astype(BF16), bias_ref[...], d))
    update(pages)

    @pl.when(step == pl.num_programs(1) - 1)
    def _():
        update([(kn_ref[0].astype(BF16), vn_ref[0].astype(BF16), biasn_ref[...], jnp.zeros((2 * rows, 1), F32))])
        lam = _diff_lambda(lam_ref, lam_init)
        on = acc_ref[...] / l_ref[...]
        o = on[0:rows] - lam * on[rows:]
        o = o * lax.rsqrt(jnp.mean(o * o, axis=-1, keepdims=True) + LN_EPS)
        o_ref[0] = o * g_ref[...] * (1.0 - lam_init)


def _attn_decode(q, k_new, v_new, cache_k, cache_v, page_table, lam_p, subln_g, *, lam_init):
    db, rows, _ = q.shape
    n_tok = rows // N_HEADS
    n_pages = page_table.shape[1]
    past_len = n_pages * PAGE_SIZE
    g_pages = PAGES_PER_STEP
    assert n_pages % g_pages == 0
    slopes = jnp.asarray(_slopes(), F32)
    row_h = jnp.tile(jnp.arange(N_HEADS), 2 * n_tok)
    row_t = jnp.tile(jnp.repeat(jnp.arange(n_tok), N_HEADS), 2)
    row_slope = slopes[row_h]
    col_pos = jnp.repeat(jnp.arange(PAGE_SIZE), N_HEADS)
    col_h = jnp.tile(jnp.arange(N_HEADS), PAGE_SIZE)
    bias = jnp.where(row_h[:, None] == col_h[None, :],
                     row_slope[:, None] * (col_pos[None, :] - row_t[:, None]).astype(F32), NEG_INF)
    ncol_t = jnp.repeat(jnp.arange(n_tok), N_HEADS)
    ncol_h = jnp.tile(jnp.arange(N_HEADS), n_tok)
    dist = row_t[:, None] - ncol_t[None, :]
    bias_new = jnp.where((row_h[:, None] == ncol_h[None, :]) & (dist >= 0),
                         -row_slope[:, None] * dist.astype(F32), NEG_INF)
    slope_col = row_slope[:, None]

    tok_spec = pl.BlockSpec((1, rows, V_DIM), lambda b, s, pt: (b, 0, 0))

    def page_spec(g):
        return pl.BlockSpec((1, PAGE_SIZE * N_HEADS, V_DIM), lambda b, s, pt: (pt[b, s * g_pages + g], 0, 0))

    def const2(shape):
        return pl.BlockSpec(shape, lambda b, s, pt: (0, 0))

    grid_spec = pltpu.PrefetchScalarGridSpec(
        num_scalar_prefetch=1,
        grid=(db, n_pages // g_pages),
        in_specs=[tok_spec, tok_spec, tok_spec]
        + [page_spec(g) for g in range(g_pages)] * 2
        + [const2(bias.shape), const2(bias_new.shape), const2(slope_col.shape),
           const2(lam_p.shape), const2(subln_g.shape)],
        out_specs=tok_spec,
        scratch_shapes=[pltpu.VMEM((2 * rows, V_DIM), BF16), pltpu.VMEM((2 * rows, 1), F32),
                        pltpu.VMEM((2 * rows, 1), F32), pltpu.VMEM((2 * rows, V_DIM), F32)],
    )
    return pl.pallas_call(
        functools.partial(_attn_decode_kernel, lam_init=lam_init, past_len=past_len, n_tok=n_tok),
        grid_spec=grid_spec,
        out_shape=jax.ShapeDtypeStruct((db, rows, V_DIM), F32),
        compiler_params=_params(2),
        name="attn_decode",
    )(page_table, q, k_new, v_new, *([cache_k] * g_pages), *([cache_v] * g_pages),
      bias, bias_new, slope_col, lam_p, subln_g)


def _mix_ffn_kernel(h_ref, o_ref, sga_ref, sgm_ref, gu_ref, vn_ref, wsp_ref, bsp_ref, wo_ref,
                    g2_ref, b2_ref, wg_ref, wu_ref, wd_ref, g3_ref, b3_ref, y_ref, s_ref, *, period):
    tm = h_ref.shape[0]
    r = lax.broadcasted_iota(jnp.int32, (CHUNK, CHUNK), 0)
    c = lax.broadcasted_iota(jnp.int32, (CHUNK, CHUNK), 1)
    keep = (r // period == c // period) & (c <= r)
    for grp in range(N_GROUPS):
        w = jnp.where(keep, wsp_ref[grp], 0.0).astype(BF16)
        cols = slice(grp * GROUP_WIDTH, (grp + 1) * GROUP_WIDTH)
        for ch in range(tm // CHUNK):
            rws = slice(ch * CHUNK, (ch + 1) * CHUNK)
            f = jnp.dot(w, vn_ref[rws, cols].astype(BF16), preferred_element_type=F32) + bsp_ref[grp]
            s_ref[rws, cols] = gu_ref[rws, cols] * f
    mix_in = sga_ref[...] * o_ref[...] + sgm_ref[...] * s_ref[...]
    mix = jnp.dot(mix_in.astype(BF16), wo_ref[...], preferred_element_type=F32)
    h2 = _layer_norm(ALPHA * h_ref[...] + mix, g2_ref[...], b2_ref[...])
    y = _swiglu(h2.astype(BF16), wg_ref, wu_ref, wd_ref)
    y_ref[...] = _layer_norm(ALPHA * h2 + 0.5 * y, g3_ref[...], b3_ref[...])


def _mix_ffn(h, o, sga, sgm, gu, vn, wsp, bsp, wo, g2, b2, wg, wu, wd, g3, b3, *, tm, period):
    rows = h.shape[0]
    row_spec = pl.BlockSpec((tm, D_MODEL), lambda i: (i, 0))
    consts = (wsp, bsp, wo, g2, b2, wg, wu, wd, g3, b3)
    return pl.pallas_call(
        functools.partial(_mix_ffn_kernel, period=period),
        grid=(rows // tm,),
        in_specs=[row_spec] * 6 + [_const_spec(a.shape) for a in consts],
        out_specs=row_spec,
        out_shape=jax.ShapeDtypeStruct((rows, D_MODEL), F32),
        scratch_shapes=[pltpu.VMEM((tm, D_MODEL), F32)],
        compiler_params=_params(1),
        name="mix_ffn",
    )(h, o, sga, sgm, gu, vn, *consts)


def _spatial_params(w_spatial, b_spatial, period):
    reps = CHUNK // period
    w = jnp.tile(w_spatial[:, :period, :period], (1, reps, reps))
    b = jnp.tile(b_spatial[:, :period], (1, reps))
    return w, jnp.broadcast_to(b[:, :, None], (N_GROUPS, CHUNK, GROUP_WIDTH))


def kernel(x_prompt, x_sample, cache_k, cache_v, page_table, ln1_g, ln1_b, ffn1_wg, ffn1_wu, ffn1_wd, w_in, lam_q1, lam_k1, lam_q2, lam_k2, subln_g, gmlp_ln_g, gmlp_ln_b, w_spatial, b_spatial, w_out, ln2_g, ln2_b, ffn2_wg, ffn2_wu, ffn2_wd, ln3_g, ln3_b):
    batch, seq, _ = x_prompt.shape
    db, n_tok, _ = x_sample.shape
    rows_p, rows_s = batch * seq, db * n_tok
    assert rows_s == CHUNK and CHUNK % n_tok == 0 and seq % ATTN_TK == 0
    hp = x_prompt.reshape(rows_p, D_MODEL)
    hs = x_sample.reshape(rows_s, D_MODEL)
    row = lambda a: a.reshape(1, -1)
    tm = ATTN_TQ
    kp_l, vp_l, ks_l, vs_l, gv_l = [], [], [], [], []
    for l in range(DEPTH):
        lam_init = 0.8 - 0.6 * math.exp(-0.3 * l)
        wg1, wu1, wd1 = ffn1_wg[l].astype(BF16), ffn1_wu[l].astype(BF16), ffn1_wd[l].astype(BF16)
        wg2, wu2, wd2 = ffn2_wg[l].astype(BF16), ffn2_wu[l].astype(BF16), ffn2_wd[l].astype(BF16)
        w_in_b, w_out_b = w_in[l].astype(BF16), w_out[l].astype(BF16)
        lam_p = jnp.stack([lam_q1[l], lam_k1[l], lam_q2[l], lam_k2[l]])
        sub_g = row(subln_g[l])
        ffn1 = (wg1, wu1, wd1, row(ln1_g[l]), row(ln1_b[l]))
        tail = (w_out_b, row(ln2_g[l]), row(ln2_b[l]), wg2, wu2, wd2, row(ln3_g[l]), row(ln3_b[l]))
        gln = (row(gmlp_ln_g[l]), row(gmlp_ln_b[l]))

        h1 = _ffn_ln(hp, *ffn1, tm=tm)
        k_p, v_p, qt, kb, vt, gu, vn, sga, sgm = _proj(h1, w_in_b, *gln, tm=tm, batch=batch)
        o = _attn_prompt(qt, kb, vt, lam_p, sub_g, lam_init=lam_init)
        wsp, bsp = _spatial_params(w_spatial[l], b_spatial[l], CHUNK)
        hp = _mix_ffn(h1, o, sga, sgm, gu, vn, wsp, bsp, *tail, tm=tm, period=CHUNK)

        h1 = _ffn_ln(hs, *ffn1, tm=rows_s)
        q_s, k_s, v_s, gu, vn_s, sga, sgm = _proj(h1, w_in_b, *gln, tm=rows_s)
        tok = lambda a: a.reshape(db, n_tok * N_HEADS, V_DIM)
        o = _attn_decode(tok(q_s), tok(k_s), tok(v_s),
                         cache_k[l].reshape(-1, PAGE_SIZE * N_HEADS, V_DIM),
                         cache_v[l].reshape(-1, PAGE_SIZE * N_HEADS, V_DIM),
                         page_table, lam_p, sub_g, lam_init=lam_init)
        wsp, bsp = _spatial_params(w_spatial[l], b_spatial[l], n_tok)
        hs = _mix_ffn(h1, o.reshape(rows_s, D_MODEL), sga, sgm, gu, vn_s, wsp, bsp, *tail, tm=rows_s, period=n_tok)

        kp_l.append(k_p.reshape(batch, seq, N_HEADS, V_DIM))
        vp_l.append(v_p.reshape(batch, seq, N_HEADS, V_DIM))
        ks_l.append(k_s.reshape(db, n_tok, N_HEADS, V_DIM))
        vs_l.append(v_s.reshape(db, n_tok, N_HEADS, V_DIM))
        gv_l.append(vn_s.reshape(db, n_tok, D_MODEL))
    return (hp.reshape(batch, seq, D_MODEL), hs.reshape(db, n_tok, D_MODEL),
            jnp.stack(kp_l), jnp.stack(vp_l), jnp.stack(ks_l), jnp.stack(vs_l), jnp.stack(gv_l))
```

```python
import functools
import math

import jax
import jax.numpy as jnp
from jax import lax
from jax.experimental import pallas as pl
from jax.experimental.pallas import tpu as pltpu

D_MODEL = 1024
DEPTH = 1
PAGE_SIZE = 128
N_HEADS = 8
HEAD_DIM = 64
V_DIM = 2 * HEAD_DIM
CHUNK = 128
GROUP_WIDTH = 128
N_GROUPS = D_MODEL // GROUP_WIDTH
D_FF = 2816
ALPHA = (2 * DEPTH) ** 0.25
LN_EPS = 1e-5
NEG_INF = -1e30
QK_SCALE = HEAD_DIM ** -0.5

VMEM_LIMIT_BYTES = 56 * 1024 * 1024
ATTN_TQ = 256
ATTN_TK = 1024
N_POS_COLS = 2
HEADS_PER_STEP = 1
PAGES_PER_STEP = 8

F32 = jnp.float32
BF16 = jnp.bfloat16


def _slopes():
    return [2.0 ** (-8.0 * (h + 1) / N_HEADS) for h in range(N_HEADS)]


def _layer_norm(x, g, b):
    mu = jnp.mean(x, axis=-1, keepdims=True)
    xc = x - mu
    var = jnp.mean(xc * xc, axis=-1, keepdims=True)
    return xc * lax.rsqrt(var + LN_EPS) * g + b


def _gelu(x):
    return 0.5 * x * (1.0 + lax.erf(x * math.sqrt(0.5)))


def _swiglu(xb, wg_ref, wu_ref, wd_ref):
    gate = jnp.dot(xb, wg_ref[...], preferred_element_type=F32)
    up = jnp.dot(xb, wu_ref[...], preferred_element_type=F32)
    act = (gate * jax.nn.sigmoid(gate)) * up
    return jnp.dot(act.astype(BF16), wd_ref[...], preferred_element_type=F32)


def _diff_lambda(lam_ref, lam_init):
    lp = lam_ref[...]
    a = jnp.sum(lp[0:1] * lp[1:2], axis=-1, keepdims=True)
    b = jnp.sum(lp[2:3] * lp[3:4], axis=-1, keepdims=True)
    return jnp.exp(a) - jnp.exp(b) + lam_init


def _const_spec(shape):
    nd = len(shape)
    return pl.BlockSpec(shape, lambda *_: (0,) * nd, pipeline_mode=pl.Buffered(1))


def _params(n_axes):
    return pltpu.CompilerParams(
        dimension_semantics=("arbitrary",) * n_axes, vmem_limit_bytes=VMEM_LIMIT_BYTES)


def _ffn_ln_kernel(x_ref, wg_ref, wu_ref, wd_ref, g_ref, b_ref, o_ref):
    x = x_ref[...]
    y = _swiglu(x.astype(BF16), wg_ref, wu_ref, wd_ref)
    o_ref[...] = _layer_norm(ALPHA * x + 0.5 * y, g_ref[...], b_ref[...])


def _ffn_ln(x, wg, wu, wd, g, b, *, tm):
    rows = x.shape[0]
    row_spec = pl.BlockSpec((tm, D_MODEL), lambda i: (i, 0))
    return pl.pallas_call(
        _ffn_ln_kernel,
        grid=(rows // tm,),
        in_specs=[row_spec, _const_spec(wg.shape), _const_spec(wu.shape), _const_spec(wd.shape),
                  _const_spec(g.shape), _const_spec(b.shape)],
        out_specs=row_spec,
        out_shape=jax.ShapeDtypeStruct((rows, D_MODEL), F32),
        compiler_params=_params(1),
        name="ffn_ln",
    )(x, wg, wu, wd, g, b)


def _proj_kernel(h_ref, w_ref, lg_ref, lb_ref, *out_refs, attn_layouts):
    hb = h_ref[...].astype(BF16)

    def section(i):
        return jnp.dot(hb, w_ref[:, i * D_MODEL:(i + 1) * D_MODEL], preferred_element_type=F32)

    q = section(0) * QK_SCALE
    k = section(1)
    v = section(2)
    if attn_layouts:
        k_ref, v_ref, qt_ref, kb_ref, vt_ref, gu_ref, vn_ref, sga_ref, sgm_ref = out_refs
        qt_ref[0, 0] = q.T.astype(BF16)
        kb_ref[...] = k.astype(BF16)
        vt_ref[0, 0] = v.T.astype(BF16)
    else:
        q_ref, k_ref, v_ref, gu_ref, vn_ref, sga_ref, sgm_ref = out_refs
        q_ref[...] = q
    k_ref[...] = k
    v_ref[...] = v
    gu_ref[...] = _gelu(section(3))
    vn_ref[...] = _layer_norm(_gelu(section(4)), lg_ref[...], lb_ref[...])
    sga_ref[...] = jax.nn.sigmoid(section(5))
    sgm_ref[...] = jax.nn.sigmoid(section(6))


def _proj(h, w_in, ln_g, ln_b, *, tm, batch=None):
    rows = h.shape[0]
    n_tiles = rows // tm
    row_spec = pl.BlockSpec((tm, D_MODEL), lambda i: (i, 0))
    row_f32 = jax.ShapeDtypeStruct((rows, D_MODEL), F32)
    attn_layouts = batch is not None
    if attn_layouts:
        per_b = n_tiles // batch
        t_spec = pl.BlockSpec((1, 1, D_MODEL, tm), lambda i: (i // per_b, i % per_b, 0, 0))
        t_shape = jax.ShapeDtypeStruct((batch, per_b, D_MODEL, tm), BF16)
        out_specs = [row_spec, row_spec, t_spec, row_spec, t_spec] + [row_spec] * 4
        out_shape = [row_f32, row_f32, t_shape, jax.ShapeDtypeStruct((rows, D_MODEL), BF16), t_shape] + [row_f32] * 4
    else:
        out_specs = [row_spec] * 7
        out_shape = [row_f32] * 7
    return pl.pallas_call(
        functools.partial(_proj_kernel, attn_layouts=attn_layouts),
        grid=(n_tiles,),
        in_specs=[row_spec, _const_spec(w_in.shape), _const_spec(ln_g.shape), _const_spec(ln_b.shape)],
        out_specs=out_specs,
        out_shape=out_shape,
        compiler_params=_params(1),
        name="proj",
    )(h, w_in, ln_g, ln_b)


def _attn_prompt_kernel(qt_ref, k_ref, vt_ref, pos_ref, rel_ref, slope_ref, lam_ref, g_ref, o_ref,
                        w_ref, acc_ref, sa_ref, sb_ref, *, lam_init):
    tq, tk, nh = ATTN_TQ, ATTN_TK, HEADS_PER_STEP
    sub = tk // tq
    qi = pl.program_id(2)
    n_full = qi // sub
    zeros = jnp.zeros((HEAD_DIM, tq), BF16)
    prow = lax.broadcasted_iota(jnp.int32, (V_DIM, 2 * tq), 0)
    ones_rows = jnp.where(prow < N_POS_COLS, 1.0, 0.0).astype(BF16)
    for h in range(nh):
        qt = qt_ref[0, 0, h * V_DIM:(h + 1) * V_DIM, :]
        w_ref[h, 0:HEAD_DIM, 0:tq] = qt[0:HEAD_DIM]
        w_ref[h, HEAD_DIM:V_DIM, 0:tq] = zeros
        w_ref[h, 0:HEAD_DIM, tq:] = zeros
        w_ref[h, HEAD_DIM:V_DIM, tq:] = qt[HEAD_DIM:]
        w_ref[h, V_DIM:, :] = ones_rows

    def head_cols(h):
        return slice(h * V_DIM, (h + 1) * V_DIM)

    def scores(j, dst_ref):
        cmax = []
        for h in range(nh):
            lhs = jnp.concatenate([k_ref[0, j, :, head_cols(h)], pos_ref[h]], axis=1)
            s = jnp.dot(lhs, w_ref[h], preferred_element_type=F32)
            dst_ref[h] = s
            cmax.append(jnp.max(s, axis=0, keepdims=True))
        return tuple(cmax)

    def absorb(j, src_ref, cmax, ml):
        out = []
        for h in range(nh):
            m, l = ml[2 * h], ml[2 * h + 1]
            d = slope_ref[h] * jnp.full((1, 2 * tq), j * tk - qi * tq, jnp.int32).astype(F32)
            m_new = jnp.maximum(m, cmax[h] + d)
            a = jnp.exp(m - m_new)
            p = jnp.exp(src_ref[h] - (m_new - d))
            vt = jnp.concatenate([vt_ref[0, j * sub + u, head_cols(h), :] for u in range(sub)], axis=1)
            acc_ref[h] = a * acc_ref[h] + jnp.dot(vt, p.astype(BF16), preferred_element_type=F32)
            out += [m_new, a * l + jnp.sum(p, axis=0, keepdims=True)]
        return tuple(out)

    def diagonal(r):
        def f():
            nk = (r + 1) * tq
            cmax0 = scores(0, sa_ref)
            out = []
            for h in range(nh):
                lhs = jnp.concatenate([k_ref[0, n_full, 0:nk, head_cols(h)], pos_ref[h, 0:nk]], axis=1)
                s = jnp.dot(lhs, w_ref[h], preferred_element_type=F32)
                tail = jnp.where(rel_ref[...] <= 0.0, s[r * tq:], NEG_INF)
                s = tail if r == 0 else jnp.concatenate([s[:r * tq], tail], axis=0)
                d = slope_ref[h] * float(-r * tq)
                m = jnp.max(s, axis=0, keepdims=True) + d
                p = jnp.exp(s - (m - d))
                vt = jnp.concatenate([vt_ref[0, n_full * sub + u, head_cols(h), :] for u in range(r + 1)], axis=1)
                acc_ref[h] = jnp.dot(vt, p.astype(BF16), preferred_element_type=F32)
                out += [m, jnp.sum(p, axis=0, keepdims=True)]
            return tuple(out) + cmax0
        return f

    state = lax.switch(qi % sub, [diagonal(r) for r in range(sub)])

    def step(j, s_cur, s_nxt, state):
        ml, cmax = state[:2 * nh], state[2 * nh:]
        out, cmax_next = [], []
        for h in range(nh):
            m, l = ml[2 * h], ml[2 * h + 1]
            d = slope_ref[h] * jnp.full((1, 2 * tq), j * tk - qi * tq, jnp.int32).astype(F32)
            m_new = jnp.maximum(m, cmax[h] + d)
            a = jnp.exp(m - m_new)
            shift = m_new - d
            cm, lsum, pv = None, None, None
            for u in range(sub):
                rows = slice(u * tq, (u + 1) * tq)
                lhs = jnp.concatenate([k_ref[0, j + 1, rows, head_cols(h)], pos_ref[h, rows]], axis=1)
                s = jnp.dot(lhs, w_ref[h], preferred_element_type=F32)
                s_nxt[h, rows] = s
                bm = jnp.max(s, axis=0, keepdims=True)
                cm = bm if cm is None else jnp.maximum(cm, bm)
                p = jnp.exp(s_cur[h, rows] - shift)
                ps = jnp.sum(p, axis=0, keepdims=True)
                lsum = ps if lsum is None else lsum + ps
                part = jnp.dot(vt_ref[0, j * sub + u, head_cols(h), :], p.astype(BF16),
                               preferred_element_type=F32)
                pv = part if pv is None else pv + part
            acc_ref[h] = a * acc_ref[h] + pv
            out += [m_new, a * l + lsum]
            cmax_next.append(cm)
        return tuple(out) + tuple(cmax_next)

    def trip(j, state):
        return lax.cond(j % 2 == 0, lambda: step(j, sa_ref, sb_ref, state), lambda: step(j, sb_ref, sa_ref, state))

    state = lax.fori_loop(0, n_full - 1, trip, state)
    ml, cmax = state[:2 * nh], state[2 * nh:]
    last = lambda s_cur: (lambda: absorb(n_full - 1, s_cur, cmax, ml))
    which = jnp.where(n_full == 0, 0, 1 + (n_full - 1) % 2)
    carry = lax.switch(which, [lambda: ml, last(sa_ref), last(sb_ref)])

    lam = _diff_lambda(lam_ref, lam_init)
    for h in range(nh):
        on = acc_ref[h] / carry[2 * h + 1]
        ot = on[:, 0:tq] - lam * on[:, tq:]
        ot = ot * lax.rsqrt(jnp.mean(ot * ot, axis=0, keepdims=True) + LN_EPS)
        o_ref[:, h * V_DIM:(h + 1) * V_DIM] = ot.T * g_ref[...] * (1.0 - lam_init)


def _attn_prompt(qt, kb, vt, lam_p, subln_g, *, lam_init):
    batch, nblk, _, tq = qt.shape
    tk, nh = ATTN_TK, HEADS_PER_STEP
    assert tq == ATTN_TQ and (nblk * tq) % tk == 0
    nchunk = nblk * tq // tk
    kb = kb.reshape(batch, nchunk, tk, D_MODEL)
    slopes = jnp.asarray(_slopes(), F32)
    c = jnp.arange(tk)
    hi = slopes[:, None] * ((c // 256) * 256).astype(F32)[None]
    lo = slopes[:, None] * (c % 256).astype(F32)[None]
    pos = jnp.stack([hi, lo] + [jnp.zeros_like(hi)] * (V_DIM - N_POS_COLS), axis=-1).astype(BF16)
    rel = (c[:tq, None] - jnp.tile(jnp.arange(tq), 2)[None, :]).astype(F32)
    slope_row = jnp.broadcast_to(slopes[:, None, None], (N_HEADS, 1, 2 * tq))
    return pl.pallas_call(
        functools.partial(_attn_prompt_kernel, lam_init=lam_init),
        grid=(batch, N_HEADS // nh, nblk),
        in_specs=[
            pl.BlockSpec((1, 1, nh * V_DIM, tq), lambda b, h, i: (b, i, h, 0)),
            pl.BlockSpec((1, nchunk, tk, nh * V_DIM), lambda b, h, i: (b, 0, 0, h)),
            pl.BlockSpec((1, nblk, nh * V_DIM, tq), lambda b, h, i: (b, 0, h, 0)),
            pl.BlockSpec((nh, tk, V_DIM), lambda b, h, i: (h, 0, 0)),
            pl.BlockSpec(rel.shape, lambda b, h, i: (0, 0), pipeline_mode=pl.Buffered(1)),
            pl.BlockSpec((nh, 1, 2 * tq), lambda b, h, i: (h, 0, 0)),
            pl.BlockSpec(lam_p.shape, lambda b, h, i: (0, 0)),
            pl.BlockSpec(subln_g.shape, lambda b, h, i: (0, 0)),
        ],
        out_specs=pl.BlockSpec((tq, nh * V_DIM), lambda b, h, i: (b * nblk + i, h)),
        out_shape=jax.ShapeDtypeStruct((batch * nblk * tq, D_MODEL), F32),
        scratch_shapes=[pltpu.VMEM((nh, 2 * V_DIM, 2 * tq), BF16), pltpu.VMEM((nh, V_DIM, 2 * tq), F32),
                        pltpu.VMEM((nh, tk, 2 * tq), F32), pltpu.VMEM((nh, tk, 2 * tq), F32)],
        compiler_params=_params(3),
        name="attn_prompt",
    )(qt, kb, vt, pos, rel, slope_row, lam_p, subln_g)


def _attn_decode_kernel(pt_ref, q_ref, kn_ref, vn_ref, *rest, lam_init, past_len, n_tok):
    g_pages = PAGES_PER_STEP
    k_refs = rest[:g_pages]
    v_refs = rest[g_pages:2 * g_pages]
    bias_ref, biasn_ref, slope_ref, lam_ref, g_ref, o_ref, qm_ref, m_ref, l_ref, acc_ref = rest[2 * g_pages:]
    del pt_ref
    step = pl.program_id(1)
    rows = n_tok * N_HEADS

    @pl.when(step == 0)
    def _():
        q = q_ref[0]
        lane = lax.broadcasted_iota(jnp.int32, q.shape, 1)
        qm_ref[0:rows, :] = jnp.where(lane < HEAD_DIM, q, 0.0).astype(BF16)
        qm_ref[rows:, :] = jnp.where(lane >= HEAD_DIM, q, 0.0).astype(BF16)
        m_ref[...] = jnp.full_like(m_ref, NEG_INF)
        l_ref[...] = jnp.zeros_like(l_ref)
        acc_ref[...] = jnp.zeros_like(acc_ref)

    def update(pages):
        qm = qm_ref[...]
        s_list, mx = [], None
        for kp, _, bias, d in pages:
            s = lax.dot_general(qm, kp, (((1,), (1,)), ((), ())), preferred_element_type=F32) + bias
            s_list.append(s)
            blk = jnp.max(s, axis=1, keepdims=True) + d
            mx = blk if mx is None else jnp.maximum(mx, blk)
        m = m_ref[...]
        m_new = jnp.maximum(m, mx)
        a = jnp.exp(m - m_new)
        l = a * l_ref[...]
        acc = a * acc_ref[...]
        for s, (_, vp, _, d) in zip(s_list, pages):
            p = jnp.exp(s - (m_new - d))
            l = l + jnp.sum(p, axis=1, keepdims=True)
            acc = acc + jnp.dot(p.astype(BF16), vp, preferred_element_type=F32)
        l_ref[...] = l
        acc_ref[...] = acc
        m_ref[...] = m_new

    slope = slope_ref[...]
    pages = []
    for g in range(g_pages):
        page = step * g_pages + g
        d = slope * jnp.full((2 * rows, 1), page * PAGE_SIZE - past_len, jnp.int32).astype(F32)
        pages.append((k_refs[g][0].astype(BF16), v_refs[g][0].astype(BF16), bias_ref[...], d))
    update(pages)

    @pl.when(step == pl.num_programs(1) - 1)
    def _():
        update([(kn_ref[0].astype(BF16), vn_ref[0].astype(BF16), biasn_ref[...], jnp.zeros((2 * rows, 1), F32))])
        lam = _diff_lambda(lam_ref, lam_init)
        on = acc_ref[...] / l_ref[...]
        o = on[0:rows] - lam * on[rows:]
        o = o * lax.rsqrt(jnp.mean(o * o, axis=-1, keepdims=True) + LN_EPS)
        o_ref[0] = o * g_ref[...] * (1.0 - lam_init)


def _attn_decode(q, k_new, v_new, cache_k, cache_v, page_table, lam_p, subln_g, *, lam_init):
    db, rows, _ = q.shape
    n_tok = rows // N_HEADS
    n_pages = page_table.shape[1]
    past_len = n_pages * PAGE_SIZE
    g_pages = PAGES_PER_STEP
    assert n_pages % g_pages == 0
    slopes = jnp.asarray(_slopes(), F32)
    row_h = jnp.tile(jnp.arange(N_HEADS), 2 * n_tok)
    row_t = jnp.tile(jnp.repeat(jnp.arange(n_tok), N_HEADS), 2)
    row_slope = slopes[row_h]
    col_pos = jnp.repeat(jnp.arange(PAGE_SIZE), N_HEADS)
    col_h = jnp.tile(jnp.arange(N_HEADS), PAGE_SIZE)
    bias = jnp.where(row_h[:, None] == col_h[None, :],
                     row_slope[:, None] * (col_pos[None, :] - row_t[:, None]).astype(F32), NEG_INF)
    ncol_t = jnp.repeat(jnp.arange(n_tok), N_HEADS)
    ncol_h = jnp.tile(jnp.arange(N_HEADS), n_tok)
    dist = row_t[:, None] - ncol_t[None, :]
    bias_new = jnp.where((row_h[:, None] == ncol_h[None, :]) & (dist >= 0),
                         -row_slope[:, None] * dist.astype(F32), NEG_INF)
    slope_col = row_slope[:, None]

    tok_spec = pl.BlockSpec((1, rows, V_DIM), lambda b, s, pt: (b, 0, 0))

    def page_spec(g):
        return pl.BlockSpec((1, PAGE_SIZE * N_HEADS, V_DIM), lambda b, s, pt: (pt[b, s * g_pages + g], 0, 0))

    def const2(shape):
        return pl.BlockSpec(shape, lambda b, s, pt: (0, 0))

    grid_spec = pltpu.PrefetchScalarGridSpec(
        num_scalar_prefetch=1,
        grid=(db, n_pages // g_pages),
        in_specs=[tok_spec, tok_spec, tok_spec]
        + [page_spec(g) for g in range(g_pages)] * 2
        + [const2(bias.shape), const2(bias_new.shape), const2(slope_col.shape),
           const2(lam_p.shape), const2(subln_g.shape)],
        out_specs=tok_spec,
        scratch_shapes=[pltpu.VMEM((2 * rows, V_DIM), BF16), pltpu.VMEM((2 * rows, 1), F32),
                        pltpu.VMEM((2 * rows, 1), F32), pltpu.VMEM((2 * rows, V_DIM), F32)],
    )
    return pl.pallas_call(
        functools.partial(_attn_decode_kernel, lam_init=lam_init, past_len=past_len, n_tok=n_tok),
        grid_spec=grid_spec,
        out_shape=jax.ShapeDtypeStruct((db, rows, V_DIM), F32),
        compiler_params=_params(2),
        name="attn_decode",
    )(page_table, q, k_new, v_new, *([cache_k] * g_pages), *([cache_v] * g_pages),
      bias, bias_new, slope_col, lam_p, subln_g)


def _mix_ffn_kernel(h_ref, o_ref, sga_ref, sgm_ref, gu_ref, vn_ref, wsp_ref, bsp_ref, wo_ref,
                    g2_ref, b2_ref, wg_ref, wu_ref, wd_ref, g3_ref, b3_ref, y_ref, s_ref, *, period):
    tm = h_ref.shape[0]
    r = lax.broadcasted_iota(jnp.int32, (CHUNK, CHUNK), 0)
    c = lax.broadcasted_iota(jnp.int32, (CHUNK, CHUNK), 1)
    keep = (r // period == c // period) & (c <= r)
    for grp in range(N_GROUPS):
        w = jnp.where(keep, wsp_ref[grp], 0.0).astype(BF16)
        cols = slice(grp * GROUP_WIDTH, (grp + 1) * GROUP_WIDTH)
        for ch in range(tm // CHUNK):
            rws = slice(ch * CHUNK, (ch + 1) * CHUNK)
            f = jnp.dot(w, vn_ref[rws, cols].astype(BF16), preferred_element_type=F32) + bsp_ref[grp]
            s_ref[rws, cols] = gu_ref[rws, cols] * f
    mix_in = sga_ref[...] * o_ref[...] + sgm_ref[...] * s_ref[...]
    mix = jnp.dot(mix_in.astype(BF16), wo_ref[...], preferred_element_type=F32)
    h2 = _layer_norm(ALPHA * h_ref[...] + mix, g2_ref[...], b2_ref[...])
    y = _swiglu(h2.astype(BF16), wg_ref, wu_ref, wd_ref)
    y_ref[...] = _layer_norm(ALPHA * h2 + 0.5 * y, g3_ref[...], b3_ref[...])


def _mix_ffn(h, o, sga, sgm, gu, vn, wsp, bsp, wo, g2, b2, wg, wu, wd, g3, b3, *, tm, period):
    rows = h.shape[0]
    row_spec = pl.BlockSpec((tm, D_MODEL), lambda i: (i, 0))
    consts = (wsp, bsp, wo, g2, b2, wg, wu, wd, g3, b3)
    return pl.pallas_call(
        functools.partial(_mix_ffn_kernel, period=period),
        grid=(rows // tm,),
        in_specs=[row_spec] * 6 + [_const_spec(a.shape) for a in consts],
        out_specs=row_spec,
        out_shape=jax.ShapeDtypeStruct((rows, D_MODEL), F32),
        scratch_shapes=[pltpu.VMEM((tm, D_MODEL), F32)],
        compiler_params=_params(1),
        name="mix_ffn",
    )(h, o, sga, sgm, gu, vn, *consts)


def _spatial_params(w_spatial, b_spatial, period):
    reps = CHUNK // period
    w = jnp.tile(w_spatial[:, :period, :period], (1, reps, reps))
    b = jnp.tile(b_spatial[:, :period], (1, reps))
    return w, jnp.broadcast_to(b[:, :, None], (N_GROUPS, CHUNK, GROUP_WIDTH))


def kernel(x_prompt, x_sample, cache_k, cache_v, page_table, ln1_g, ln1_b, ffn1_wg, ffn1_wu, ffn1_wd, w_in, lam_q1, lam_k1, lam_q2, lam_k2, subln_g, gmlp_ln_g, gmlp_ln_b, w_spatial, b_spatial, w_out, ln2_g, ln2_b, ffn2_wg, ffn2_wu, ffn2_wd, ln3_g, ln3_b):
    batch, seq, _ = x_prompt.shape
    db, n_tok, _ = x_sample.shape
    rows_p, rows_s = batch * seq, db * n_tok
    assert rows_s == CHUNK and CHUNK % n_tok == 0 and seq % ATTN_TK == 0
    hp = x_prompt.reshape(rows_p, D_MODEL)
    hs = x_sample.reshape(rows_s, D_MODEL)
    row = lambda a: a.reshape(1, -1)
    tm = ATTN_TQ
    kp_l, vp_l, ks_l, vs_l, gv_l = [], [], [], [], []
    for l in range(DEPTH):
        lam_init = 0.8 - 0.6 * math.exp(-0.3 * l)
        wg1, wu1, wd1 = ffn1_wg[l].astype(BF16), ffn1_wu[l].astype(BF16), ffn1_wd[l].astype(BF16)
        wg2, wu2, wd2 = ffn2_wg[l].astype(BF16), ffn2_wu[l].astype(BF16), ffn2_wd[l].astype(BF16)
        w_in_b, w_out_b = w_in[l].astype(BF16), w_out[l].astype(BF16)
        lam_p = jnp.stack([lam_q1[l], lam_k1[l], lam_q2[l], lam_k2[l]])
        sub_g = row(subln_g[l])
        ffn1 = (wg1, wu1, wd1, row(ln1_g[l]), row(ln1_b[l]))
        tail = (w_out_b, row(ln2_g[l]), row(ln2_b[l]), wg2, wu2, wd2, row(ln3_g[l]), row(ln3_b[l]))
        gln = (row(gmlp_ln_g[l]), row(gmlp_ln_b[l]))

        h1 = _ffn_ln(hp, *ffn1, tm=tm)
        k_p, v_p, qt, kb, vt, gu, vn, sga, sgm = _proj(h1, w_in_b, *gln, tm=tm, batch=batch)
        o = _attn_prompt(qt, kb, vt, lam_p, sub_g, lam_init=lam_init)
        wsp, bsp = _spatial_params(w_spatial[l], b_spatial[l], CHUNK)
        hp = _mix_ffn(h1, o, sga, sgm, gu, vn, wsp, bsp, *tail, tm=tm, period=CHUNK)

        h1 = _ffn_ln(hs, *ffn1, tm=rows_s)
        q_s, k_s, v_s, gu, vn_s, sga, sgm = _proj(h1, w_in_b, *gln, tm=rows_s)
        tok = lambda a: a.reshape(db, n_tok * N_HEADS, V_DIM)
        o = _attn_decode(tok(q_s), tok(k_s), tok(v_s),
                         cache_k[l].reshape(-1, PAGE_SIZE * N_HEADS, V_DIM),
                         cache_v[l].reshape(-1, PAGE_SIZE * N_HEADS, V_DIM),
                         page_table, lam_p, sub_g, lam_init=lam_init)
        wsp, bsp = _spatial_params(w_spatial[l], b_spatial[l], n_tok)
        hs = _mix_ffn(h1, o.reshape(rows_s, D_MODEL), sga, sgm, gu, vn_s, wsp, bsp, *tail, tm=rows_s, period=n_tok)

        kp_l.append(k_p.reshape(batch, seq, N_HEADS, V_DIM))
        vp_l.append(v_p.reshape(batch, seq, N_HEADS, V_DIM))
        ks_l.append(k_s.reshape(db, n_tok, N_HEADS, V_DIM))
        vs_l.append(v_s.reshape(db, n_tok, N_HEADS, V_DIM))
        gv_l.append(vn_s.reshape(db, n_tok, D_MODEL))
    return (hp.reshape(batch, seq, D_MODEL), hs.reshape(db, n_tok, D_MODEL),
            jnp.stack(kp_l), jnp.stack(vp_l), jnp.stack(ks_l), jnp.stack(vs_l), jnp.stack(gv_l))
```

```python
import functools
import math

import jax
import jax.numpy as jnp
import numpy as np
from jax import lax
from jax.experimental import pallas as pl
from jax.experimental.pallas import tpu as pltpu

D_MODEL = 1024
DEPTH = 1
PAGE_SIZE = 128
N_HEADS = 8
HEAD_DIM = 64
V_DIM = 2 * HEAD_DIM
CHUNK = 128
GROUP_WIDTH = 128
N_GROUPS = D_MODEL // GROUP_WIDTH
D_FF = 2816
ALPHA = (2 * DEPTH) ** 0.25
LN_EPS = 1e-5
NEG_INF = -1e30
QK_SCALE = HEAD_DIM ** -0.5

VMEM_LIMIT_BYTES = 56 * 1024 * 1024
ATTN_TQ = 256
ATTN_TK = 1024
N_POS_COLS = 2
HEADS_PER_STEP = 2
PAGES_PER_STEP = 16

F32 = jnp.float32
BF16 = jnp.bfloat16


def _slopes():
    return [2.0 ** (-8.0 * (h + 1) / N_HEADS) for h in range(N_HEADS)]


def _layer_norm(x, g, b):
    mu = jnp.mean(x, axis=-1, keepdims=True)
    xc = x - mu
    var = jnp.mean(xc * xc, axis=-1, keepdims=True)
    return xc * lax.rsqrt(var + LN_EPS) * g + b


def _gelu(x):
    return 0.5 * x * (1.0 + lax.erf(x * math.sqrt(0.5)))


def _swiglu(xb, wg_ref, wu_ref, wd_ref):
    gate = jnp.dot(xb, wg_ref[...], preferred_element_type=F32)
    up = jnp.dot(xb, wu_ref[...], preferred_element_type=F32)
    act = (gate * jax.nn.sigmoid(gate)) * up
    return jnp.dot(act.astype(BF16), wd_ref[...], preferred_element_type=F32)


def _diff_lambda(lam_ref, lam_init):
    lp = lam_ref[...]
    a = jnp.sum(lp[0:1] * lp[1:2], axis=-1, keepdims=True)
    b = jnp.sum(lp[2:3] * lp[3:4], axis=-1, keepdims=True)
    return jnp.exp(a) - jnp.exp(b) + lam_init


def _const_spec(shape):
    nd = len(shape)
    return pl.BlockSpec(shape, lambda *_: (0,) * nd, pipeline_mode=pl.Buffered(1))


def _params(n_axes):
    return pltpu.CompilerParams(
        dimension_semantics=("arbitrary",) * n_axes, vmem_limit_bytes=VMEM_LIMIT_BYTES)


def _ffn_ln_kernel(x_ref, wg_ref, wu_ref, wd_ref, g_ref, b_ref, o_ref):
    x = x_ref[...]
    y = _swiglu(x.astype(BF16), wg_ref, wu_ref, wd_ref)
    o_ref[...] = _layer_norm(ALPHA * x + 0.5 * y, g_ref[...], b_ref[...])


def _ffn_ln(x, wg, wu, wd, g, b, *, tm):
    rows = x.shape[0]
    row_spec = pl.BlockSpec((tm, D_MODEL), lambda i: (i, 0))
    return pl.pallas_call(
        _ffn_ln_kernel,
        grid=(rows // tm,),
        in_specs=[row_spec, _const_spec(wg.shape), _const_spec(wu.shape), _const_spec(wd.shape),
                  _const_spec(g.shape), _const_spec(b.shape)],
        out_specs=row_spec,
        out_shape=jax.ShapeDtypeStruct((rows, D_MODEL), F32),
        compiler_params=_params(1),
        name="ffn_ln",
    )(x, wg, wu, wd, g, b)


def _proj_kernel(h_ref, w_ref, lg_ref, lb_ref, *out_refs, attn_layouts):
    hb = h_ref[...].astype(BF16)

    def section(i):
        return jnp.dot(hb, w_ref[:, i * D_MODEL:(i + 1) * D_MODEL], preferred_element_type=F32)

    q = section(0) * QK_SCALE
    k = section(1)
    v = section(2)
    if attn_layouts:
        k_ref, v_ref, qt_ref, kb_ref, vt_ref, gu_ref, vn_ref, sga_ref, sgm_ref = out_refs
        qt_ref[0, 0] = q.T.astype(BF16)
        kb_ref[...] = k.astype(BF16)
        vt_ref[0, 0] = v.T.astype(BF16)
    else:
        q_ref, k_ref, v_ref, gu_ref, vn_ref, sga_ref, sgm_ref = out_refs
        q_ref[...] = q
    k_ref[...] = k
    v_ref[...] = v
    gu_ref[...] = _gelu(section(3))
    vn_ref[...] = _layer_norm(_gelu(section(4)), lg_ref[...], lb_ref[...])
    sga_ref[...] = jax.nn.sigmoid(section(5))
    sgm_ref[...] = jax.nn.sigmoid(section(6))


def _proj(h, w_in, ln_g, ln_b, *, tm, batch=None):
    rows = h.shape[0]
    n_tiles = rows // tm
    row_spec = pl.BlockSpec((tm, D_MODEL), lambda i: (i, 0))
    row_f32 = jax.ShapeDtypeStruct((rows, D_MODEL), F32)
    attn_layouts = batch is not None
    if attn_layouts:
        per_b = n_tiles // batch
        t_spec = pl.BlockSpec((1, 1, D_MODEL, tm), lambda i: (i // per_b, i % per_b, 0, 0))
        t_shape = jax.ShapeDtypeStruct((batch, per_b, D_MODEL, tm), BF16)
        out_specs = [row_spec, row_spec, t_spec, row_spec, t_spec] + [row_spec] * 4
        out_shape = [row_f32, row_f32, t_shape, jax.ShapeDtypeStruct((rows, D_MODEL), BF16), t_shape] + [row_f32] * 4
    else:
        out_specs = [row_spec] * 7
        out_shape = [row_f32] * 7
    return pl.pallas_call(
        functools.partial(_proj_kernel, attn_layouts=attn_layouts),
        grid=(n_tiles,),
        in_specs=[row_spec, _const_spec(w_in.shape), _const_spec(ln_g.shape), _const_spec(ln_b.shape)],
        out_specs=out_specs,
        out_shape=out_shape,
        compiler_params=_params(1),
        name="proj",
    )(h, w_in, ln_g, ln_b)


def _attn_prompt_kernel(qt_ref, k_ref, vt_ref, pos_ref, rel_ref, slope_ref, lam_ref, g_ref, o_ref,
                        w_ref, acc_ref, sa_ref, sb_ref, *, lam_init):
    tq, tk, nh = ATTN_TQ, ATTN_TK, HEADS_PER_STEP
    sub = tk // tq
    qi = pl.program_id(2)
    n_full = qi // sub
    zeros = jnp.zeros((HEAD_DIM, tq), BF16)
    prow = lax.broadcasted_iota(jnp.int32, (V_DIM, 2 * tq), 0)
    ones_rows = jnp.where(prow < N_POS_COLS, 1.0, 0.0).astype(BF16)
    for h in range(nh):
        qt = qt_ref[0, 0, h * V_DIM:(h + 1) * V_DIM, :]
        w_ref[h, 0:HEAD_DIM, 0:tq] = qt[0:HEAD_DIM]
        w_ref[h, HEAD_DIM:V_DIM, 0:tq] = zeros
        w_ref[h, 0:HEAD_DIM, tq:] = zeros
        w_ref[h, HEAD_DIM:V_DIM, tq:] = qt[HEAD_DIM:]
        w_ref[h, V_DIM:, :] = ones_rows

    def head_cols(h):
        return slice(h * V_DIM, (h + 1) * V_DIM)

    def scores(j, dst_ref):
        cmax = []
        for h in range(nh):
            lhs = jnp.concatenate([k_ref[0, j, :, head_cols(h)], pos_ref[h]], axis=1)
            s = jnp.dot(lhs, w_ref[h], preferred_element_type=F32)
            dst_ref[h] = s
            cmax.append(jnp.max(s, axis=0, keepdims=True))
        return tuple(cmax)

    def absorb(j, src_ref, cmax, ml):
        out = []
        for h in range(nh):
            m, l = ml[2 * h], ml[2 * h + 1]
            d = slope_ref[h] * jnp.full((1, 2 * tq), j * tk - qi * tq, jnp.int32).astype(F32)
            m_new = jnp.maximum(m, cmax[h] + d)
            a = jnp.exp(m - m_new)
            p = jnp.exp(src_ref[h] - (m_new - d))
            vt = jnp.concatenate([vt_ref[0, j * sub + u, head_cols(h), :] for u in range(sub)], axis=1)
            acc_ref[h] = a * acc_ref[h] + jnp.dot(vt, p.astype(BF16), preferred_element_type=F32)
            out += [m_new, a * l + jnp.sum(p, axis=0, keepdims=True)]
        return tuple(out)

    def diagonal(r):
        def f():
            nk = (r + 1) * tq
            cmax0 = scores(0, sa_ref)
            out = []
            for h in range(nh):
                lhs = jnp.concatenate([k_ref[0, n_full, 0:nk, head_cols(h)], pos_ref[h, 0:nk]], axis=1)
                s = jnp.dot(lhs, w_ref[h], preferred_element_type=F32)
                tail = jnp.where(rel_ref[...] <= 0.0, s[r * tq:], NEG_INF)
                s = tail if r == 0 else jnp.concatenate([s[:r * tq], tail], axis=0)
                d = slope_ref[h] * float(-r * tq)
                m = jnp.max(s, axis=0, keepdims=True) + d
                p = jnp.exp(s - (m - d))
                vt = jnp.concatenate([vt_ref[0, n_full * sub + u, head_cols(h), :] for u in range(r + 1)], axis=1)
                acc_ref[h] = jnp.dot(vt, p.astype(BF16), preferred_element_type=F32)
                out += [m, jnp.sum(p, axis=0, keepdims=True)]
            return tuple(out) + cmax0
        return f

    state = lax.switch(qi % sub, [diagonal(r) for r in range(sub)])

    def step(j, s_cur, s_nxt, state):
        ml, cmax = state[:2 * nh], state[2 * nh:]
        out, cmax_next = [], []
        for h in range(nh):
            m, l = ml[2 * h], ml[2 * h + 1]
            d = slope_ref[h] * jnp.full((1, 2 * tq), j * tk - qi * tq, jnp.int32).astype(F32)
            m_new = jnp.maximum(m, cmax[h] + d)
            a = jnp.exp(m - m_new)
            shift = m_new - d
            cm, lsum, pv = None, None, None
            for u in range(sub):
                rows = slice(u * tq, (u + 1) * tq)
                lhs = jnp.concatenate([k_ref[0, j + 1, rows, head_cols(h)], pos_ref[h, rows]], axis=1)
                s = jnp.dot(lhs, w_ref[h], preferred_element_type=F32)
                s_nxt[h, rows] = s
                bm = jnp.max(s, axis=0, keepdims=True)
                cm = bm if cm is None else jnp.maximum(cm, bm)
                p = jnp.exp(s_cur[h, rows] - shift)
                ps = jnp.sum(p, axis=0, keepdims=True)
                lsum = ps if lsum is None else lsum + ps
                part = jnp.dot(vt_ref[0, j * sub + u, head_cols(h), :], p.astype(BF16),
                               preferred_element_type=F32)
                pv = part if pv is None else pv + part
            acc_ref[h] = a * acc_ref[h] + pv
            out += [m_new, a * l + lsum]
            cmax_next.append(cm)
        return tuple(out) + tuple(cmax_next)

    def trip(j, state):
        return lax.cond(j % 2 == 0, lambda: step(j, sa_ref, sb_ref, state), lambda: step(j, sb_ref, sa_ref, state))

    state = lax.fori_loop(0, n_full - 1, trip, state)
    ml, cmax = state[:2 * nh], state[2 * nh:]
    last = lambda s_cur: (lambda: absorb(n_full - 1, s_cur, cmax, ml))
    which = jnp.where(n_full == 0, 0, 1 + (n_full - 1) % 2)
    carry = lax.switch(which, [lambda: ml, last(sa_ref), last(sb_ref)])

    lam = _diff_lambda(lam_ref, lam_init)
    for h in range(nh):
        on = acc_ref[h] / carry[2 * h + 1]
        ot = on[:, 0:tq] - lam * on[:, tq:]
        ot = ot * lax.rsqrt(jnp.mean(ot * ot, axis=0, keepdims=True) + LN_EPS)
        o_ref[:, h * V_DIM:(h + 1) * V_DIM] = ot.T * g_ref[...] * (1.0 - lam_init)


def _attn_prompt(qt, kb, vt, lam_p, subln_g, *, lam_init):
    batch, nblk, _, tq = qt.shape
    tk, nh = ATTN_TK, HEADS_PER_STEP
    assert tq == ATTN_TQ and (nblk * tq) % tk == 0
    nchunk = nblk * tq // tk
    kb = kb.reshape(batch, nchunk, tk, D_MODEL)
    slopes = np.asarray(_slopes(), np.float32)
    c = np.arange(tk)
    pos = np.zeros((N_HEADS, tk, V_DIM), np.float32)
    pos[:, :, 0] = slopes[:, None] * ((c // 256) * 256)[None]
    pos[:, :, 1] = slopes[:, None] * (c % 256)[None]
    pos = jnp.asarray(pos, BF16)
    rel = (c[:tq, None] - np.tile(np.arange(tq), 2)[None, :]).astype(np.float32)
    slope_row = np.ascontiguousarray(np.broadcast_to(slopes[:, None, None], (N_HEADS, 1, 2 * tq)))
    return pl.pallas_call(
        functools.partial(_attn_prompt_kernel, lam_init=lam_init),
        grid=(batch, N_HEADS // nh, nblk),
        in_specs=[
            pl.BlockSpec((1, 1, nh * V_DIM, tq), lambda b, h, i: (b, i, h, 0)),
            pl.BlockSpec((1, nchunk, tk, nh * V_DIM), lambda b, h, i: (b, 0, 0, h)),
            pl.BlockSpec((1, nblk, nh * V_DIM, tq), lambda b, h, i: (b, 0, h, 0)),
            pl.BlockSpec((nh, tk, V_DIM), lambda b, h, i: (h, 0, 0)),
            pl.BlockSpec(rel.shape, lambda b, h, i: (0, 0), pipeline_mode=pl.Buffered(1)),
            pl.BlockSpec((nh, 1, 2 * tq), lambda b, h, i: (h, 0, 0)),
            pl.BlockSpec(lam_p.shape, lambda b, h, i: (0, 0)),
            pl.BlockSpec(subln_g.shape, lambda b, h, i: (0, 0)),
        ],
        out_specs=pl.BlockSpec((tq, nh * V_DIM), lambda b, h, i: (b * nblk + i, h)),
        out_shape=jax.ShapeDtypeStruct((batch * nblk * tq, D_MODEL), F32),
        scratch_shapes=[pltpu.VMEM((nh, 2 * V_DIM, 2 * tq), BF16), pltpu.VMEM((nh, V_DIM, 2 * tq), F32),
                        pltpu.VMEM((nh, tk, 2 * tq), F32), pltpu.VMEM((nh, tk, 2 * tq), F32)],
        compiler_params=_params(3),
        name="attn_prompt",
    )(qt, kb, vt, pos, rel, slope_row, lam_p, subln_g)


def _attn_decode_kernel(pt_ref, q_ref, kn_ref, vn_ref, *rest, lam_init, past_len, n_tok):
    g_pages = PAGES_PER_STEP
    k_refs = rest[:g_pages]
    v_refs = rest[g_pages:2 * g_pages]
    bias_ref, biasn_ref, slope_ref, lam_ref, g_ref, o_ref, qm_ref, m_ref, l_ref, acc_ref = rest[2 * g_pages:]
    del pt_ref
    step = pl.program_id(1)
    rows = n_tok * N_HEADS

    @pl.when(step == 0)
    def _():
        q = q_ref[0]
        lane = lax.broadcasted_iota(jnp.int32, q.shape, 1)
        qm_ref[0:rows, :] = jnp.where(lane < HEAD_DIM, q, 0.0).astype(BF16)
        qm_ref[rows:, :] = jnp.where(lane >= HEAD_DIM, q, 0.0).astype(BF16)
        m_ref[...] = jnp.full_like(m_ref, NEG_INF)
        l_ref[...] = jnp.zeros_like(l_ref)
        acc_ref[...] = jnp.zeros_like(acc_ref)

    def update(pages):
        qm = qm_ref[...]
        s_list, mx = [], None
        for kp, _, bias, d in pages:
            s = lax.dot_general(qm, kp, (((1,), (1,)), ((), ())), preferred_element_type=F32) + bias
            s_list.append(s)
            blk = jnp.max(s, axis=1, keepdims=True) + d
            mx = blk if mx is None else jnp.maximum(mx, blk)
        m = m_ref[...]
        m_new = jnp.maximum(m, mx)
        a = jnp.exp(m - m_new)
        l = a * l_ref[...]
        acc = a * acc_ref[...]
        for s, (_, vp, _, d) in zip(s_list, pages):
            p = jnp.exp(s - (m_new - d))
            l = l + jnp.sum(p, axis=1, keepdims=True)
            acc = acc + jnp.dot(p.astype(BF16), vp, preferred_element_type=F32)
        l_ref[...] = l
        acc_ref[...] = acc
        m_ref[...] = m_new

    slope = slope_ref[...]
    pages = []
    for g in range(g_pages):
        page = step * g_pages + g
        d = slope * jnp.full((2 * rows, 1), page * PAGE_SIZE - past_len, jnp.int32).astype(F32)
        pages.append((k_refs[g][0].astype(BF16), v_refs[g][0].astype(BF16), bias_ref[...], d))
    update(pages)

    @pl.when(step == pl.num_programs(1) - 1)
    def _():
        update([(kn_ref[0].astype(BF16), vn_ref[0].astype(BF16), biasn_ref[...], jnp.zeros((2 * rows, 1), F32))])
        lam = _diff_lambda(lam_ref, lam_init)
        on = acc_ref[...] / l_ref[...]
        o = on[0:rows] - lam * on[rows:]
        o = o * lax.rsqrt(jnp.mean(o * o, axis=-1, keepdims=True) + LN_EPS)
        o_ref[0] = o * g_ref[...] * (1.0 - lam_init)


def _attn_decode(q, k_new, v_new, cache_k, cache_v, page_table, lam_p, subln_g, *, lam_init):
    db, rows, _ = q.shape
    n_tok = rows // N_HEADS
    n_pages = page_table.shape[1]
    past_len = n_pages * PAGE_SIZE
    g_pages = PAGES_PER_STEP
    assert n_pages % g_pages == 0
    slopes = np.asarray(_slopes(), np.float32)
    row_h = np.tile(np.arange(N_HEADS), 2 * n_tok)
    row_t = np.tile(np.repeat(np.arange(n_tok), N_HEADS), 2)
    row_slope = slopes[row_h]
    col_pos = np.repeat(np.arange(PAGE_SIZE), N_HEADS)
    col_h = np.tile(np.arange(N_HEADS), PAGE_SIZE)
    bias = np.where(row_h[:, None] == col_h[None, :],
                    row_slope[:, None] * (col_pos[None, :] - row_t[:, None]), NEG_INF).astype(np.float32)
    ncol_t = np.repeat(np.arange(n_tok), N_HEADS)
    ncol_h = np.tile(np.arange(N_HEADS), n_tok)
    dist = row_t[:, None] - ncol_t[None, :]
    bias_new = np.where((row_h[:, None] == ncol_h[None, :]) & (dist >= 0),
                        -row_slope[:, None] * dist, NEG_INF).astype(np.float32)
    slope_col = np.ascontiguousarray(row_slope[:, None])

    tok_spec = pl.BlockSpec((1, rows, V_DIM), lambda b, s, pt: (b, 0, 0))

    def page_spec(g):
        return pl.BlockSpec((1, PAGE_SIZE * N_HEADS, V_DIM), lambda b, s, pt: (pt[b, s * g_pages + g], 0, 0))

    def const2(shape):
        return pl.BlockSpec(shape, lambda b, s, pt: (0, 0))

    grid_spec = pltpu.PrefetchScalarGridSpec(
        num_scalar_prefetch=1,
        grid=(db, n_pages // g_pages),
        in_specs=[tok_spec, tok_spec, tok_spec]
        + [page_spec(g) for g in range(g_pages)] * 2
        + [const2(bias.shape), const2(bias_new.shape), const2(slope_col.shape),
           const2(lam_p.shape), const2(subln_g.shape)],
        out_specs=tok_spec,
        scratch_shapes=[pltpu.VMEM((2 * rows, V_DIM), BF16), pltpu.VMEM((2 * rows, 1), F32),
                        pltpu.VMEM((2 * rows, 1), F32), pltpu.VMEM((2 * rows, V_DIM), F32)],
    )
    return pl.pallas_call(
        functools.partial(_attn_decode_kernel, lam_init=lam_init, past_len=past_len, n_tok=n_tok),
        grid_spec=grid_spec,
        out_shape=jax.ShapeDtypeStruct((db, rows, V_DIM), F32),
        compiler_params=_params(2),
        name="attn_decode",
    )(page_table, q, k_new, v_new, *([cache_k] * g_pages), *([cache_v] * g_pages),
      bias, bias_new, slope_col, lam_p, subln_g)


def _mix_ffn_kernel(h_ref, o_ref, sga_ref, sgm_ref, gu_ref, vn_ref, wsp_ref, bsp_ref, wo_ref,
                    g2_ref, b2_ref, wg_ref, wu_ref, wd_ref, g3_ref, b3_ref, y_ref, s_ref, *, period):
    tm = h_ref.shape[0]
    r = lax.broadcasted_iota(jnp.int32, (CHUNK, CHUNK), 0)
    c = lax.broadcasted_iota(jnp.int32, (CHUNK, CHUNK), 1)
    keep = (r // period == c // period) & (c <= r)
    for grp in range(N_GROUPS):
        w = jnp.where(keep, wsp_ref[grp], 0.0).astype(BF16)
        cols = slice(grp * GROUP_WIDTH, (grp + 1) * GROUP_WIDTH)
        for ch in range(tm // CHUNK):
            rws = slice(ch * CHUNK, (ch + 1) * CHUNK)
            f = jnp.dot(w, vn_ref[rws, cols].astype(BF16), preferred_element_type=F32) + bsp_ref[grp]
            s_ref[rws, cols] = gu_ref[rws, cols] * f
    mix_in = sga_ref[...] * o_ref[...] + sgm_ref[...] * s_ref[...]
    mix = jnp.dot(mix_in.astype(BF16), wo_ref[...], preferred_element_type=F32)
    h2 = _layer_norm(ALPHA * h_ref[...] + mix, g2_ref[...], b2_ref[...])
    y = _swiglu(h2.astype(BF16), wg_ref, wu_ref, wd_ref)
    y_ref[...] = _layer_norm(ALPHA * h2 + 0.5 * y, g3_ref[...], b3_ref[...])


def _mix_ffn(h, o, sga, sgm, gu, vn, wsp, bsp, wo, g2, b2, wg, wu, wd, g3, b3, *, tm, period):
    rows = h.shape[0]
    row_spec = pl.BlockSpec((tm, D_MODEL), lambda i: (i, 0))
    consts = (wsp, bsp, wo, g2, b2, wg, wu, wd, g3, b3)
    return pl.pallas_call(
        functools.partial(_mix_ffn_kernel, period=period),
        grid=(rows // tm,),
        in_specs=[row_spec] * 6 + [_const_spec(a.shape) for a in consts],
        out_specs=row_spec,
        out_shape=jax.ShapeDtypeStruct((rows, D_MODEL), F32),
        scratch_shapes=[pltpu.VMEM((tm, D_MODEL), F32)],
        compiler_params=_params(1),
        name="mix_ffn",
    )(h, o, sga, sgm, gu, vn, *consts)


def _spatial_params(w_spatial, b_spatial, period):
    reps = CHUNK // period
    w = jnp.tile(w_spatial[:, :period, :period], (1, reps, reps))
    b = jnp.tile(b_spatial[:, :period], (1, reps))
    return w, jnp.broadcast_to(b[:, :, None], (N_GROUPS, CHUNK, GROUP_WIDTH))


def kernel(x_prompt, x_sample, cache_k, cache_v, page_table, ln1_g, ln1_b, ffn1_wg, ffn1_wu, ffn1_wd, w_in, lam_q1, lam_k1, lam_q2, lam_k2, subln_g, gmlp_ln_g, gmlp_ln_b, w_spatial, b_spatial, w_out, ln2_g, ln2_b, ffn2_wg, ffn2_wu, ffn2_wd, ln3_g, ln3_b):
    batch, seq, _ = x_prompt.shape
    db, n_tok, _ = x_sample.shape
    rows_p, rows_s = batch * seq, db * n_tok
    assert rows_s == CHUNK and CHUNK % n_tok == 0 and seq % ATTN_TK == 0
    hp = x_prompt.reshape(rows_p, D_MODEL)
    hs = x_sample.reshape(rows_s, D_MODEL)
    row = lambda a: a.reshape(1, -1)
    tm = ATTN_TQ
    kp_l, vp_l, ks_l, vs_l, gv_l = [], [], [], [], []
    for l in range(DEPTH):
        lam_init = 0.8 - 0.6 * math.exp(-0.3 * l)
        wg1, wu1, wd1 = ffn1_wg[l].astype(BF16), ffn1_wu[l].astype(BF16), ffn1_wd[l].astype(BF16)
        wg2, wu2, wd2 = ffn2_wg[l].astype(BF16), ffn2_wu[l].astype(BF16), ffn2_wd[l].astype(BF16)
        w_in_b, w_out_b = w_in[l].astype(BF16), w_out[l].astype(BF16)
        lam_p = jnp.stack([lam_q1[l], lam_k1[l], lam_q2[l], lam_k2[l]])
        sub_g = row(subln_g[l])
        ffn1 = (wg1, wu1, wd1, row(ln1_g[l]), row(ln1_b[l]))
        tail = (w_out_b, row(ln2_g[l]), row(ln2_b[l]), wg2, wu2, wd2, row(ln3_g[l]), row(ln3_b[l]))
        gln = (row(gmlp_ln_g[l]), row(gmlp_ln_b[l]))

        h1 = _ffn_ln(hp, *ffn1, tm=tm)
        k_p, v_p, qt, kb, vt, gu, vn, sga, sgm = _proj(h1, w_in_b, *gln, tm=tm, batch=batch)
        o = _attn_prompt(qt, kb, vt, lam_p, sub_g, lam_init=lam_init)
        wsp, bsp = _spatial_params(w_spatial[l], b_spatial[l], CHUNK)
        hp = _mix_ffn(h1, o, sga, sgm, gu, vn, wsp, bsp, *tail, tm=tm, period=CHUNK)

        h1 = _ffn_ln(hs, *ffn1, tm=rows_s)
        q_s, k_s, v_s, gu, vn_s, sga, sgm = _proj(h1, w_in_b, *gln, tm=rows_s)
        tok = lambda a: a.reshape(db, n_tok * N_HEADS, V_DIM)
        o = _attn_decode(tok(q_s), tok(k_s), tok(v_s),
                         cache_k[l].reshape(-1, PAGE_SIZE * N_HEADS, V_DIM),
                         cache_v[l].reshape(-1, PAGE_SIZE * N_HEADS, V_DIM),
                         page_table, lam_p, sub_g, lam_init=lam_init)
        wsp, bsp = _spatial_params(w_spatial[l], b_spatial[l], n_tok)
        hs = _mix_ffn(h1, o.reshape(rows_s, D_MODEL), sga, sgm, gu, vn_s, wsp, bsp, *tail, tm=rows_s, period=n_tok)

        kp_l.append(k_p.reshape(batch, seq, N_HEADS, V_DIM))
        vp_l.append(v_p.reshape(batch, seq, N_HEADS, V_DIM))
        ks_l.append(k_s.reshape(db, n_tok, N_HEADS, V_DIM))
        vs_l.append(v_s.reshape(db, n_tok, N_HEADS, V_DIM))
        gv_l.append(vn_s.reshape(db, n_tok, D_MODEL))
    return (hp.reshape(batch, seq, D_MODEL), hs.reshape(db, n_tok, D_MODEL),
            jnp.stack(kp_l), jnp.stack(vp_l), jnp.stack(ks_l), jnp.stack(vs_l), jnp.stack(gv_l))
```

```python
import functools
import math

import jax
import jax.numpy as jnp
import numpy as np
from jax import lax
from jax.experimental import pallas as pl
from jax.experimental.pallas import tpu as pltpu

D_MODEL = 1024
DEPTH = 1
PAGE_SIZE = 128
N_HEADS = 8
HEAD_DIM = 64
V_DIM = 2 * HEAD_DIM
CHUNK = 128
GROUP_WIDTH = 128
N_GROUPS = D_MODEL // GROUP_WIDTH
D_FF = 2816
ALPHA = (2 * DEPTH) ** 0.25
LN_EPS = 1e-5
NEG_INF = -1e30
QK_SCALE = HEAD_DIM ** -0.5

VMEM_LIMIT_BYTES = 56 * 1024 * 1024
ATTN_TQ = 256
ATTN_TK = 1024
N_POS_COLS = 2
HEADS_PER_STEP = 2
PAGES_PER_STEP = 16

F32 = jnp.float32
BF16 = jnp.bfloat16


def _slopes():
    return [2.0 ** (-8.0 * (h + 1) / N_HEADS) for h in range(N_HEADS)]


def _layer_norm(x, g, b):
    mu = jnp.mean(x, axis=-1, keepdims=True)
    xc = x - mu
    var = jnp.mean(xc * xc, axis=-1, keepdims=True)
    return xc * lax.rsqrt(var + LN_EPS) * g + b


def _gelu(x):
    return 0.5 * x * (1.0 + lax.erf(x * math.sqrt(0.5)))


def _swiglu(xb, wg_ref, wu_ref, wd_ref):
    gate = jnp.dot(xb, wg_ref[...], preferred_element_type=F32)
    up = jnp.dot(xb, wu_ref[...], preferred_element_type=F32)
    act = (gate * jax.nn.sigmoid(gate)) * up
    return jnp.dot(act.astype(BF16), wd_ref[...], preferred_element_type=F32)


def _diff_lambda(lam_ref, lam_init):
    lp = lam_ref[...]
    a = jnp.sum(lp[0:1] * lp[1:2], axis=-1, keepdims=True)
    b = jnp.sum(lp[2:3] * lp[3:4], axis=-1, keepdims=True)
    return jnp.exp(a) - jnp.exp(b) + lam_init


def _const_spec(shape):
    nd = len(shape)
    return pl.BlockSpec(shape, lambda *_: (0,) * nd, pipeline_mode=pl.Buffered(1))


def _params(n_axes):
    return pltpu.CompilerParams(
        dimension_semantics=("arbitrary",) * n_axes, vmem_limit_bytes=VMEM_LIMIT_BYTES)


def _ffn_ln_kernel(x_ref, wg_ref, wu_ref, wd_ref, g_ref, b_ref, o_ref):
    x = x_ref[...]
    y = _swiglu(x.astype(BF16), wg_ref, wu_ref, wd_ref)
    o_ref[...] = _layer_norm(ALPHA * x + 0.5 * y, g_ref[...], b_ref[...])


def _ffn_ln(x, wg, wu, wd, g, b, *, tm):
    rows = x.shape[0]
    row_spec = pl.BlockSpec((tm, D_MODEL), lambda i: (i, 0))
    return pl.pallas_call(
        _ffn_ln_kernel,
        grid=(rows // tm,),
        in_specs=[row_spec, _const_spec(wg.shape), _const_spec(wu.shape), _const_spec(wd.shape),
                  _const_spec(g.shape), _const_spec(b.shape)],
        out_specs=row_spec,
        out_shape=jax.ShapeDtypeStruct((rows, D_MODEL), F32),
        compiler_params=_params(1),
        name="ffn_ln",
    )(x, wg, wu, wd, g, b)


def _proj_kernel(h_ref, w_ref, lg_ref, lb_ref, *out_refs, attn_layouts):
    hb = h_ref[...].astype(BF16)

    def section(i):
        return jnp.dot(hb, w_ref[:, i * D_MODEL:(i + 1) * D_MODEL], preferred_element_type=F32)

    q = section(0) * QK_SCALE
    k = section(1)
    v = section(2)
    if attn_layouts:
        k_ref, v_ref, qt_ref, kb_ref, vt_ref, gu_ref, vn_ref, sga_ref, sgm_ref = out_refs
        qt_ref[0, 0] = q.T.astype(BF16)
        kb_ref[...] = k.astype(BF16)
        vt_ref[0, 0] = v.T.astype(BF16)
    else:
        q_ref, k_ref, v_ref, gu_ref, vn_ref, sga_ref, sgm_ref = out_refs
        q_ref[...] = q
    k_ref[...] = k
    v_ref[...] = v
    gu_ref[...] = _gelu(section(3))
    vn_ref[...] = _layer_norm(_gelu(section(4)), lg_ref[...], lb_ref[...])
    sga_ref[...] = jax.nn.sigmoid(section(5))
    sgm_ref[...] = jax.nn.sigmoid(section(6))


def _proj(h, w_in, ln_g, ln_b, *, tm, batch=None):
    rows = h.shape[0]
    n_tiles = rows // tm
    row_spec = pl.BlockSpec((tm, D_MODEL), lambda i: (i, 0))
    row_f32 = jax.ShapeDtypeStruct((rows, D_MODEL), F32)
    attn_layouts = batch is not None
    if attn_layouts:
        per_b = n_tiles // batch
        t_spec = pl.BlockSpec((1, 1, D_MODEL, tm), lambda i: (i // per_b, i % per_b, 0, 0))
        t_shape = jax.ShapeDtypeStruct((batch, per_b, D_MODEL, tm), BF16)
        out_specs = [row_spec, row_spec, t_spec, row_spec, t_spec] + [row_spec] * 4
        out_shape = [row_f32, row_f32, t_shape, jax.ShapeDtypeStruct((rows, D_MODEL), BF16), t_shape] + [row_f32] * 4
    else:
        out_specs = [row_spec] * 7
        out_shape = [row_f32] * 7
    return pl.pallas_call(
        functools.partial(_proj_kernel, attn_layouts=attn_layouts),
        grid=(n_tiles,),
        in_specs=[row_spec, _const_spec(w_in.shape), _const_spec(ln_g.shape), _const_spec(ln_b.shape)],
        out_specs=out_specs,
        out_shape=out_shape,
        compiler_params=_params(1),
        name="proj",
    )(h, w_in, ln_g, ln_b)


def _attn_prompt_kernel(qt_ref, k_ref, vt_ref, pos_ref, rel_ref, slope_ref, lam_ref, g_ref, o_ref,
                        w_ref, acc_ref, sa_ref, sb_ref, *, lam_init):
    tq, tk, nh = ATTN_TQ, ATTN_TK, HEADS_PER_STEP
    sub = tk // tq
    qi = pl.program_id(2)
    n_full = qi // sub
    zeros = jnp.zeros((HEAD_DIM, tq), BF16)
    prow = lax.broadcasted_iota(jnp.int32, (V_DIM, 2 * tq), 0)
    ones_rows = jnp.where(prow < N_POS_COLS, 1.0, 0.0).astype(BF16)
    for h in range(nh):
        qt = qt_ref[0, 0, h * V_DIM:(h + 1) * V_DIM, :]
        w_ref[h, 0:HEAD_DIM, 0:tq] = qt[0:HEAD_DIM]
        w_ref[h, HEAD_DIM:V_DIM, 0:tq] = zeros
        w_ref[h, 0:HEAD_DIM, tq:] = zeros
        w_ref[h, HEAD_DIM:V_DIM, tq:] = qt[HEAD_DIM:]
        w_ref[h, V_DIM:, :] = ones_rows

    def head_cols(h):
        return slice(h * V_DIM, (h + 1) * V_DIM)

    def scores(j, dst_ref):
        cmax = []
        for h in range(nh):
            lhs = jnp.concatenate([k_ref[0, j, :, head_cols(h)], pos_ref[h]], axis=1)
            s = jnp.dot(lhs, w_ref[h], preferred_element_type=F32)
            dst_ref[h] = s
            cmax.append(jnp.max(s, axis=0, keepdims=True))
        return tuple(cmax)

    def absorb(j, src_ref, cmax, ml):
        out = []
        for h in range(nh):
            m, l = ml[2 * h], ml[2 * h + 1]
            d = slope_ref[h] * jnp.full((1, 2 * tq), j * tk - qi * tq, jnp.int32).astype(F32)
            m_new = jnp.maximum(m, cmax[h] + d)
            a = jnp.exp(m - m_new)
            p = jnp.exp(src_ref[h] - (m_new - d))
            vt = jnp.concatenate([vt_ref[0, j * sub + u, head_cols(h), :] for u in range(sub)], axis=1)
            acc_ref[h] = a * acc_ref[h] + jnp.dot(vt, p.astype(BF16), preferred_element_type=F32)
            out += [m_new, a * l + jnp.sum(p, axis=0, keepdims=True)]
        return tuple(out)

    def diagonal(r):
        def f():
            nk = (r + 1) * tq
            cm = [None] * nh

            def scores0(u):
                rows = slice(u * tq, (u + 1) * tq)
                for h in range(nh):
                    lhs = jnp.concatenate([k_ref[0, 0, rows, head_cols(h)], pos_ref[h, rows]], axis=1)
                    s0 = jnp.dot(lhs, w_ref[h], preferred_element_type=F32)
                    sa_ref[h, rows] = s0
                    bm = jnp.max(s0, axis=0, keepdims=True)
                    cm[h] = bm if cm[h] is None else jnp.maximum(cm[h], bm)

            s_d = []
            for h in range(nh):
                lhs = jnp.concatenate([k_ref[0, n_full, 0:nk, head_cols(h)], pos_ref[h, 0:nk]], axis=1)
                s_d.append(jnp.dot(lhs, w_ref[h], preferred_element_type=F32))
            scores0(0)
            ms, ds = [], []
            for h in range(nh):
                tail = jnp.where(rel_ref[...] <= 0.0, s_d[h][r * tq:], NEG_INF)
                s_d[h] = tail if r == 0 else jnp.concatenate([s_d[h][:r * tq], tail], axis=0)
                d = slope_ref[h] * float(-r * tq)
                ms.append(jnp.max(s_d[h], axis=0, keepdims=True) + d)
                ds.append(d)
            scores0(1)
            ps = [jnp.exp(s_d[h] - (ms[h] - ds[h])) for h in range(nh)]
            scores0(2)
            out = []
            for h in range(nh):
                vt = jnp.concatenate([vt_ref[0, n_full * sub + u, head_cols(h), :] for u in range(r + 1)], axis=1)
                acc_ref[h] = jnp.dot(vt, ps[h].astype(BF16), preferred_element_type=F32)
                out += [ms[h], jnp.sum(ps[h], axis=0, keepdims=True)]
            scores0(3)
            return tuple(out) + tuple(cm)
        return f

    state = lax.switch(qi % sub, [diagonal(r) for r in range(sub)])

    def step(j, s_cur, s_nxt, state):
        ml, cmax = state[:2 * nh], state[2 * nh:]
        out, cmax_next = [], []
        for h in range(nh):
            m, l = ml[2 * h], ml[2 * h + 1]
            d = slope_ref[h] * jnp.full((1, 2 * tq), j * tk - qi * tq, jnp.int32).astype(F32)
            m_new = jnp.maximum(m, cmax[h] + d)
            a = jnp.exp(m - m_new)
            shift = m_new - d
            cm, lsum, pv = None, None, None
            for u in range(sub):
                rows = slice(u * tq, (u + 1) * tq)
                lhs = jnp.concatenate([k_ref[0, j + 1, rows, head_cols(h)], pos_ref[h, rows]], axis=1)
                s = jnp.dot(lhs, w_ref[h], preferred_element_type=F32)
                s_nxt[h, rows] = s
                bm = jnp.max(s, axis=0, keepdims=True)
                cm = bm if cm is None else jnp.maximum(cm, bm)
                p = jnp.exp(s_cur[h, rows] - shift)
                ps = jnp.sum(p, axis=0, keepdims=True)
                lsum = ps if lsum is None else lsum + ps
                part = jnp.dot(vt_ref[0, j * sub + u, head_cols(h), :], p.astype(BF16),
                               preferred_element_type=F32)
                pv = part if pv is None else pv + part
            acc_ref[h] = a * acc_ref[h] + pv
            out += [m_new, a * l + lsum]
            cmax_next.append(cm)
        return tuple(out) + tuple(cmax_next)

    def trip(j, state):
        return lax.cond(j % 2 == 0, lambda: step(j, sa_ref, sb_ref, state), lambda: step(j, sb_ref, sa_ref, state))

    state = lax.fori_loop(0, n_full - 1, trip, state)
    ml, cmax = state[:2 * nh], state[2 * nh:]
    last = lambda s_cur: (lambda: absorb(n_full - 1, s_cur, cmax, ml))
    which = jnp.where(n_full == 0, 0, 1 + (n_full - 1) % 2)
    carry = lax.switch(which, [lambda: ml, last(sa_ref), last(sb_ref)])

    lam = _diff_lambda(lam_ref, lam_init)
    for h in range(nh):
        on = acc_ref[h] / carry[2 * h + 1]
        ot = on[:, 0:tq] - lam * on[:, tq:]
        ot = ot * lax.rsqrt(jnp.mean(ot * ot, axis=0, keepdims=True) + LN_EPS)
        o_ref[:, h * V_DIM:(h + 1) * V_DIM] = ot.T * g_ref[...] * (1.0 - lam_init)


def _attn_prompt(qt, kb, vt, lam_p, subln_g, *, lam_init):
    batch, nblk, _, tq = qt.shape
    tk, nh = ATTN_TK, HEADS_PER_STEP
    assert tq == ATTN_TQ and (nblk * tq) % tk == 0
    nchunk = nblk * tq // tk
    kb = kb.reshape(batch, nchunk, tk, D_MODEL)
    slopes = np.asarray(_slopes(), np.float32)
    c = np.arange(tk)
    pos = np.zeros((N_HEADS, tk, V_DIM), np.float32)
    pos[:, :, 0] = slopes[:, None] * ((c // 256) * 256)[None]
    pos[:, :, 1] = slopes[:, None] * (c % 256)[None]
    pos = jnp.asarray(pos, BF16)
    rel = (c[:tq, None] - np.tile(np.arange(tq), 2)[None, :]).astype(np.float32)
    slope_row = np.ascontiguousarray(np.broadcast_to(slopes[:, None, None], (N_HEADS, 1, 2 * tq)))
    return pl.pallas_call(
        functools.partial(_attn_prompt_kernel, lam_init=lam_init),
        grid=(batch, N_HEADS // nh, nblk),
        in_specs=[
            pl.BlockSpec((1, 1, nh * V_DIM, tq), lambda b, h, i: (b, i, h, 0)),
            pl.BlockSpec((1, nchunk, tk, nh * V_DIM), lambda b, h, i: (b, 0, 0, h)),
            pl.BlockSpec((1, nblk, nh * V_DIM, tq), lambda b, h, i: (b, 0, h, 0)),
            pl.BlockSpec((nh, tk, V_DIM), lambda b, h, i: (h, 0, 0)),
            pl.BlockSpec(rel.shape, lambda b, h, i: (0, 0), pipeline_mode=pl.Buffered(1)),
            pl.BlockSpec((nh, 1, 2 * tq), lambda b, h, i: (h, 0, 0)),
            pl.BlockSpec(lam_p.shape, lambda b, h, i: (0, 0)),
            pl.BlockSpec(subln_g.shape, lambda b, h, i: (0, 0)),
        ],
        out_specs=pl.BlockSpec((tq, nh * V_DIM), lambda b, h, i: (b * nblk + i, h)),
        out_shape=jax.ShapeDtypeStruct((batch * nblk * tq, D_MODEL), F32),
        scratch_shapes=[pltpu.VMEM((nh, 2 * V_DIM, 2 * tq), BF16), pltpu.VMEM((nh, V_DIM, 2 * tq), F32),
                        pltpu.VMEM((nh, tk, 2 * tq), F32), pltpu.VMEM((nh, tk, 2 * tq), F32)],
        compiler_params=_params(3),
        name="attn_prompt",
    )(qt, kb, vt, pos, rel, slope_row, lam_p, subln_g)


def _attn_decode_kernel(pt_ref, q_ref, kn_ref, vn_ref, *rest, lam_init, past_len, n_tok):
    g_pages = PAGES_PER_STEP
    k_refs = rest[:g_pages]
    v_refs = rest[g_pages:2 * g_pages]
    bias_ref, biasn_ref, slope_ref, lam_ref, g_ref, o_ref, qm_ref, m_ref, l_ref, acc_ref = rest[2 * g_pages:]
    del pt_ref
    step = pl.program_id(1)
    rows = n_tok * N_HEADS

    @pl.when(step == 0)
    def _():
        q = q_ref[0]
        lane = lax.broadcasted_iota(jnp.int32, q.shape, 1)
        qm_ref[0:rows, :] = jnp.where(lane < HEAD_DIM, q, 0.0).astype(BF16)
        qm_ref[rows:, :] = jnp.where(lane >= HEAD_DIM, q, 0.0).astype(BF16)
        m_ref[...] = jnp.full_like(m_ref, NEG_INF)
        l_ref[...] = jnp.zeros_like(l_ref)
        acc_ref[...] = jnp.zeros_like(acc_ref)

    def update(pages):
        qm = qm_ref[...]
        s_list, mx = [], None
        for kp, _, bias, d in pages:
            s = lax.dot_general(qm, kp, (((1,), (1,)), ((), ())), preferred_element_type=F32) + bias
            s_list.append(s)
            blk = jnp.max(s, axis=1, keepdims=True) + d
            mx = blk if mx is None else jnp.maximum(mx, blk)
        m = m_ref[...]
        m_new = jnp.maximum(m, mx)
        a = jnp.exp(m - m_new)
        l = a * l_ref[...]
        acc = a * acc_ref[...]
        for s, (_, vp, _, d) in zip(s_list, pages):
            p = jnp.exp(s - (m_new - d))
            l = l + jnp.sum(p, axis=1, keepdims=True)
            acc = acc + jnp.dot(p.astype(BF16), vp, preferred_element_type=F32)
        l_ref[...] = l
        acc_ref[...] = acc
        m_ref[...] = m_new

    slope = slope_ref[...]
    pages = []
    for g in range(g_pages):
        page = step * g_pages + g
        d = slope * jnp.full((2 * rows, 1), page * PAGE_SIZE - past_len, jnp.int32).astype(F32)
        pages.append((k_refs[g][0].astype(BF16), v_refs[g][0].astype(BF16), bias_ref[...], d))
    update(pages)

    @pl.when(step == pl.num_programs(1) - 1)
    def _():
        update([(kn_ref[0].astype(BF16), vn_ref[0].astype(BF16), biasn_ref[...], jnp.zeros((2 * rows, 1), F32))])
        lam = _diff_lambda(lam_ref, lam_init)
        on = acc_ref[...] / l_ref[...]
        o = on[0:rows] - lam * on[rows:]
        o = o * lax.rsqrt(jnp.mean(o * o, axis=-1, keepdims=True) + LN_EPS)
        o_ref[0] = o * g_ref[...] * (1.0 - lam_init)


def _attn_decode(q, k_new, v_new, cache_k, cache_v, page_table, lam_p, subln_g, *, lam_init):
    db, rows, _ = q.shape
    n_tok = rows // N_HEADS
    n_pages = page_table.shape[1]
    past_len = n_pages * PAGE_SIZE
    g_pages = PAGES_PER_STEP
    assert n_pages % g_pages == 0
    slopes = np.asarray(_slopes(), np.float32)
    row_h = np.tile(np.arange(N_HEADS), 2 * n_tok)
    row_t = np.tile(np.repeat(np.arange(n_tok), N_HEADS), 2)
    row_slope = slopes[row_h]
    col_pos = np.repeat(np.arange(PAGE_SIZE), N_HEADS)
    col_h = np.tile(np.arange(N_HEADS), PAGE_SIZE)
    bias = np.where(row_h[:, None] == col_h[None, :],
                    row_slope[:, None] * (col_pos[None, :] - row_t[:, None]), NEG_INF).astype(np.float32)
    ncol_t = np.repeat(np.arange(n_tok), N_HEADS)
    ncol_h = np.tile(np.arange(N_HEADS), n_tok)
    dist = row_t[:, None] - ncol_t[None, :]
    bias_new = np.where((row_h[:, None] == ncol_h[None, :]) & (dist >= 0),
                        -row_slope[:, None] * dist, NEG_INF).astype(np.float32)
    slope_col = np.ascontiguousarray(row_slope[:, None])

    tok_spec = pl.BlockSpec((1, rows, V_DIM), lambda b, s, pt: (b, 0, 0))

    def page_spec(g):
        return pl.BlockSpec((1, PAGE_SIZE * N_HEADS, V_DIM), lambda b, s, pt: (pt[b, s * g_pages + g], 0, 0))

    def const2(shape):
        return pl.BlockSpec(shape, lambda b, s, pt: (0, 0))

    grid_spec = pltpu.PrefetchScalarGridSpec(
        num_scalar_prefetch=1,
        grid=(db, n_pages // g_pages),
        in_specs=[tok_spec, tok_spec, tok_spec]
        + [page_spec(g) for g in range(g_pages)] * 2
        + [const2(bias.shape), const2(bias_new.shape), const2(slope_col.shape),
           const2(lam_p.shape), const2(subln_g.shape)],
        out_specs=tok_spec,
        scratch_shapes=[pltpu.VMEM((2 * rows, V_DIM), BF16), pltpu.VMEM((2 * rows, 1), F32),
                        pltpu.VMEM((2 * rows, 1), F32), pltpu.VMEM((2 * rows, V_DIM), F32)],
    )
    return pl.pallas_call(
        functools.partial(_attn_decode_kernel, lam_init=lam_init, past_len=past_len, n_tok=n_tok),
        grid_spec=grid_spec,
        out_shape=jax.ShapeDtypeStruct((db, rows, V_DIM), F32),
        compiler_params=_params(2),
        name="attn_decode",
    )(page_table, q, k_new, v_new, *([cache_k] * g_pages), *([cache_v] * g_pages),
      bias, bias_new, slope_col, lam_p, subln_g)


def _mix_ffn_kernel(h_ref, o_ref, sga_ref, sgm_ref, gu_ref, vn_ref, wsp_ref, bsp_ref, wo_ref,
                    g2_ref, b2_ref, wg_ref, wu_ref, wd_ref, g3_ref, b3_ref, y_ref, s_ref, *, period):
    tm = h_ref.shape[0]
    r = lax.broadcasted_iota(jnp.int32, (CHUNK, CHUNK), 0)
    c = lax.broadcasted_iota(jnp.int32, (CHUNK, CHUNK), 1)
    keep = (r // period == c // period) & (c <= r)
    for grp in range(N_GROUPS):
        w = jnp.where(keep, wsp_ref[grp], 0.0).astype(BF16)
        cols = slice(grp * GROUP_WIDTH, (grp + 1) * GROUP_WIDTH)
        for ch in range(tm // CHUNK):
            rws = slice(ch * CHUNK, (ch + 1) * CHUNK)
            f = jnp.dot(w, vn_ref[rws, cols].astype(BF16), preferred_element_type=F32) + bsp_ref[grp]
            s_ref[rws, cols] = gu_ref[rws, cols] * f
    mix_in = sga_ref[...] * o_ref[...] + sgm_ref[...] * s_ref[...]
    mix = jnp.dot(mix_in.astype(BF16), wo_ref[...], preferred_element_type=F32)
    h2 = _layer_norm(ALPHA * h_ref[...] + mix, g2_ref[...], b2_ref[...])
    y = _swiglu(h2.astype(BF16), wg_ref, wu_ref, wd_ref)
    y_ref[...] = _layer_norm(ALPHA * h2 + 0.5 * y, g3_ref[...], b3_ref[...])


def _mix_ffn(h, o, sga, sgm, gu, vn, wsp, bsp, wo, g2, b2, wg, wu, wd, g3, b3, *, tm, period):
    rows = h.shape[0]
    row_spec = pl.BlockSpec((tm, D_MODEL), lambda i: (i, 0))
    consts = (wsp, bsp, wo, g2, b2, wg, wu, wd, g3, b3)
    return pl.pallas_call(
        functools.partial(_mix_ffn_kernel, period=period),
        grid=(rows // tm,),
        in_specs=[row_spec] * 6 + [_const_spec(a.shape) for a in consts],
        out_specs=row_spec,
        out_shape=jax.ShapeDtypeStruct((rows, D_MODEL), F32),
        scratch_shapes=[pltpu.VMEM((tm, D_MODEL), F32)],
        compiler_params=_params(1),
        name="mix_ffn",
    )(h, o, sga, sgm, gu, vn, *consts)


def _spatial_params(w_spatial, b_spatial, period):
    if period == CHUNK:
        w, b = w_spatial, b_spatial
    else:
        sel = np.equal.outer(np.arange(CHUNK) % period, np.arange(period)).astype(np.float32)
        hi = lax.Precision.HIGHEST
        w = jnp.einsum("ra,gab,cb->grc", sel, w_spatial[:, :period, :period], sel, precision=hi)
        b = jnp.einsum("ra,ga->gr", sel, b_spatial[:, :period], precision=hi)
    return w, b[:, :, None]


def kernel(x_prompt, x_sample, cache_k, cache_v, page_table, ln1_g, ln1_b, ffn1_wg, ffn1_wu, ffn1_wd, w_in, lam_q1, lam_k1, lam_q2, lam_k2, subln_g, gmlp_ln_g, gmlp_ln_b, w_spatial, b_spatial, w_out, ln2_g, ln2_b, ffn2_wg, ffn2_wu, ffn2_wd, ln3_g, ln3_b):
    batch, seq, _ = x_prompt.shape
    db, n_tok, _ = x_sample.shape
    rows_p, rows_s = batch * seq, db * n_tok
    assert rows_s == CHUNK and CHUNK % n_tok == 0 and seq % ATTN_TK == 0
    hp = x_prompt.reshape(rows_p, D_MODEL)
    hs = x_sample.reshape(rows_s, D_MODEL)
    row = lambda a: a.reshape(1, -1)
    tm = ATTN_TQ
    kp_l, vp_l, ks_l, vs_l, gv_l = [], [], [], [], []
    for l in range(DEPTH):
        lam_init = 0.8 - 0.6 * math.exp(-0.3 * l)
        wg1, wu1, wd1 = ffn1_wg[l].astype(BF16), ffn1_wu[l].astype(BF16), ffn1_wd[l].astype(BF16)
        wg2, wu2, wd2 = ffn2_wg[l].astype(BF16), ffn2_wu[l].astype(BF16), ffn2_wd[l].astype(BF16)
        w_in_b, w_out_b = w_in[l].astype(BF16), w_out[l].astype(BF16)
        lam_p = jnp.stack([lam_q1[l], lam_k1[l], lam_q2[l], lam_k2[l]])
        sub_g = row(subln_g[l])
        ffn1 = (wg1, wu1, wd1, row(ln1_g[l]), row(ln1_b[l]))
        tail = (w_out_b, row(ln2_g[l]), row(ln2_b[l]), wg2, wu2, wd2, row(ln3_g[l]), row(ln3_b[l]))
        gln = (row(gmlp_ln_g[l]), row(gmlp_ln_b[l]))

        h1 = _ffn_ln(hp, *ffn1, tm=tm)
        k_p, v_p, qt, kb, vt, gu, vn, sga, sgm = _proj(h1, w_in_b, *gln, tm=tm, batch=batch)
        o = _attn_prompt(qt, kb, vt, lam_p, sub_g, lam_init=lam_init)
        wsp, bsp = _spatial_params(w_spatial[l], b_spatial[l], CHUNK)
        hp = _mix_ffn(h1, o, sga, sgm, gu, vn, wsp, bsp, *tail, tm=tm, period=CHUNK)

        h1 = _ffn_ln(hs, *ffn1, tm=rows_s)
        q_s, k_s, v_s, gu, vn_s, sga, sgm = _proj(h1, w_in_b, *gln, tm=rows_s)
        tok = lambda a: a.reshape(db, n_tok * N_HEADS, V_DIM)
        o = _attn_decode(tok(q_s), tok(k_s), tok(v_s),
                         cache_k[l].reshape(-1, PAGE_SIZE * N_HEADS, V_DIM),
                         cache_v[l].reshape(-1, PAGE_SIZE * N_HEADS, V_DIM),
                         page_table, lam_p, sub_g, lam_init=lam_init)
        wsp, bsp = _spatial_params(w_spatial[l], b_spatial[l], n_tok)
        hs = _mix_ffn(h1, o.reshape(rows_s, D_MODEL), sga, sgm, gu, vn_s, wsp, bsp, *tail, tm=rows_s, period=n_tok)

        kp_l.append(k_p.reshape(batch, seq, N_HEADS, V_DIM))
        vp_l.append(v_p.reshape(batch, seq, N_HEADS, V_DIM))
        ks_l.append(k_s.reshape(db, n_tok, N_HEADS, V_DIM))
        vs_l.append(v_s.reshape(db, n_tok, N_HEADS, V_DIM))
        gv_l.append(vn_s.reshape(db, n_tok, D_MODEL))
    return (hp.reshape(batch, seq, D_MODEL), hs.reshape(db, n_tok, D_MODEL),
            jnp.stack(kp_l), jnp.stack(vp_l), jnp.stack(ks_l), jnp.stack(vs_l), jnp.stack(gv_l))
```

```python
import functools
import math

import jax
import jax.numpy as jnp
import numpy as np
from jax import lax
from jax.experimental import pallas as pl
from jax.experimental.pallas import tpu as pltpu

D_MODEL = 1024
DEPTH = 1
PAGE_SIZE = 128
N_HEADS = 8
HEAD_DIM = 64
V_DIM = 2 * HEAD_DIM
CHUNK = 128
GROUP_WIDTH = 128
N_GROUPS = D_MODEL // GROUP_WIDTH
D_FF = 2816
ALPHA = (2 * DEPTH) ** 0.25
LN_EPS = 1e-5
NEG_INF = -1e30
QK_SCALE = HEAD_DIM ** -0.5

VMEM_LIMIT_BYTES = 56 * 1024 * 1024
ATTN_TQ = 256
ATTN_TK = 1024
N_POS_COLS = 2
HEADS_PER_STEP = 2
PAGES_PER_STEP = 16

F32 = jnp.float32
BF16 = jnp.bfloat16


def _slopes():
    return [2.0 ** (-8.0 * (h + 1) / N_HEADS) for h in range(N_HEADS)]


def _layer_norm(x, g, b):
    mu = jnp.mean(x, axis=-1, keepdims=True)
    xc = x - mu
    var = jnp.mean(xc * xc, axis=-1, keepdims=True)
    return xc * lax.rsqrt(var + LN_EPS) * g + b


def _gelu(x):
    return 0.5 * x * (1.0 + lax.erf(x * math.sqrt(0.5)))


def _swiglu(xb, wg_ref, wu_ref, wd_ref):
    gate = jnp.dot(xb, wg_ref[...], preferred_element_type=F32)
    up = jnp.dot(xb, wu_ref[...], preferred_element_type=F32)
    act = (gate * jax.nn.sigmoid(gate)) * up
    return jnp.dot(act.astype(BF16), wd_ref[...], preferred_element_type=F32)


def _diff_lambda(lam_ref, lam_init):
    lp = lam_ref[...]
    a = jnp.sum(lp[0:1] * lp[1:2], axis=-1, keepdims=True)
    b = jnp.sum(lp[2:3] * lp[3:4], axis=-1, keepdims=True)
    return jnp.exp(a) - jnp.exp(b) + lam_init


def _const_spec(shape):
    nd = len(shape)
    return pl.BlockSpec(shape, lambda *_: (0,) * nd, pipeline_mode=pl.Buffered(1))


def _params(n_axes):
    return pltpu.CompilerParams(
        dimension_semantics=("arbitrary",) * n_axes, vmem_limit_bytes=VMEM_LIMIT_BYTES)


def _ffn_ln_kernel(x_ref, wg_ref, wu_ref, wd_ref, g_ref, b_ref, o_ref):
    x = x_ref[...]
    y = _swiglu(x.astype(BF16), wg_ref, wu_ref, wd_ref)
    o_ref[...] = _layer_norm(ALPHA * x + 0.5 * y, g_ref[...], b_ref[...])


def _ffn_ln(x, wg, wu, wd, g, b, *, tm):
    rows = x.shape[0]
    row_spec = pl.BlockSpec((tm, D_MODEL), lambda i: (i, 0))
    return pl.pallas_call(
        _ffn_ln_kernel,
        grid=(rows // tm,),
        in_specs=[row_spec, _const_spec(wg.shape), _const_spec(wu.shape), _const_spec(wd.shape),
                  _const_spec(g.shape), _const_spec(b.shape)],
        out_specs=row_spec,
        out_shape=jax.ShapeDtypeStruct((rows, D_MODEL), F32),
        compiler_params=_params(1),
        name="ffn_ln",
    )(x, wg, wu, wd, g, b)


def _proj_kernel(h_ref, w_ref, lg_ref, lb_ref, *out_refs, attn_layouts):
    hb = h_ref[...].astype(BF16)

    def section(i):
        return jnp.dot(hb, w_ref[:, i * D_MODEL:(i + 1) * D_MODEL], preferred_element_type=F32)

    q = section(0) * QK_SCALE
    k = section(1)
    v = section(2)
    if attn_layouts:
        k_ref, v_ref, qt_ref, kb_ref, vt_ref, gu_ref, vn_ref, sga_ref, sgm_ref = out_refs
        qt_ref[0, 0] = q.T.astype(BF16)
        kb_ref[...] = k.astype(BF16)
        vt_ref[0, 0] = v.T.astype(BF16)
    else:
        q_ref, k_ref, v_ref, gu_ref, vn_ref, sga_ref, sgm_ref = out_refs
        q_ref[...] = q
    k_ref[...] = k
    v_ref[...] = v
    gu_ref[...] = _gelu(section(3))
    vn_ref[...] = _layer_norm(_gelu(section(4)), lg_ref[...], lb_ref[...])
    sga_ref[...] = jax.nn.sigmoid(section(5))
    sgm_ref[...] = jax.nn.sigmoid(section(6))


def _proj(h, w_in, ln_g, ln_b, *, tm, batch=None):
    rows = h.shape[0]
    n_tiles = rows // tm
    row_spec = pl.BlockSpec((tm, D_MODEL), lambda i: (i, 0))
    row_f32 = jax.ShapeDtypeStruct((rows, D_MODEL), F32)
    attn_layouts = batch is not None
    if attn_layouts:
        per_b = n_tiles // batch
        t_spec = pl.BlockSpec((1, 1, D_MODEL, tm), lambda i: (i // per_b, i % per_b, 0, 0))
        t_shape = jax.ShapeDtypeStruct((batch, per_b, D_MODEL, tm), BF16)
        out_specs = [row_spec, row_spec, t_spec, row_spec, t_spec] + [row_spec] * 4
        out_shape = [row_f32, row_f32, t_shape, jax.ShapeDtypeStruct((rows, D_MODEL), BF16), t_shape] + [row_f32] * 4
    else:
        out_specs = [row_spec] * 7
        out_shape = [row_f32] * 7
    return pl.pallas_call(
        functools.partial(_proj_kernel, attn_layouts=attn_layouts),
        grid=(n_tiles,),
        in_specs=[row_spec, _const_spec(w_in.shape), _const_spec(ln_g.shape), _const_spec(ln_b.shape)],
        out_specs=out_specs,
        out_shape=out_shape,
        compiler_params=_params(1),
        name="proj",
    )(h, w_in, ln_g, ln_b)


def _attn_prompt_kernel(qt_ref, k_ref, vt_ref, pos_ref, rel_ref, slope_ref, lam_ref, g_ref, o_ref,
                        w_ref, acc_ref, s_ref, *, lam_init):
    tq, tk, nh = ATTN_TQ, ATTN_TK, HEADS_PER_STEP
    sub = tk // tq
    qi = pl.program_id(2)
    n_full = qi // sub
    zeros = jnp.zeros((HEAD_DIM, tq), BF16)
    prow = lax.broadcasted_iota(jnp.int32, (V_DIM, 2 * tq), 0)
    ones_rows = jnp.where(prow < N_POS_COLS, 1.0, 0.0).astype(BF16)
    for h in range(nh):
        qt = qt_ref[0, 0, h * V_DIM:(h + 1) * V_DIM, :]
        w_ref[h, 0:HEAD_DIM, 0:tq] = qt[0:HEAD_DIM]
        w_ref[h, HEAD_DIM:V_DIM, 0:tq] = zeros
        w_ref[h, 0:HEAD_DIM, tq:] = zeros
        w_ref[h, HEAD_DIM:V_DIM, tq:] = qt[HEAD_DIM:]
        w_ref[h, V_DIM:, :] = ones_rows

    def head_cols(h):
        return slice(h * V_DIM, (h + 1) * V_DIM)

    def absorb(j, src_ref, cmax, ml):
        out = []
        for h in range(nh):
            m, l = ml[2 * h], ml[2 * h + 1]
            d = slope_ref[h] * jnp.full((1, 2 * tq), j * tk - qi * tq, jnp.int32).astype(F32)
            m_new = jnp.maximum(m, cmax[h] + d)
            a = jnp.exp(m - m_new)
            p = jnp.exp(src_ref[h] - (m_new - d))
            vt = jnp.concatenate([vt_ref[0, j * sub + u, head_cols(h), :] for u in range(sub)], axis=1)
            acc_ref[h] = a * acc_ref[h] + jnp.dot(vt, p.astype(BF16), preferred_element_type=F32)
            out += [m_new, a * l + jnp.sum(p, axis=0, keepdims=True)]
        return tuple(out)

    def diagonal(r):
        def f():
            nk = (r + 1) * tq
            cm = [None] * nh

            def scores0(u):
                rows = slice(u * tq, (u + 1) * tq)
                for h in range(nh):
                    lhs = jnp.concatenate([k_ref[0, 0, rows, head_cols(h)], pos_ref[h, rows]], axis=1)
                    s0 = jnp.dot(lhs, w_ref[h], preferred_element_type=F32)
                    s_ref[h, rows] = s0
                    bm = jnp.max(s0, axis=0, keepdims=True)
                    cm[h] = bm if cm[h] is None else jnp.maximum(cm[h], bm)

            s_d = []
            for h in range(nh):
                lhs = jnp.concatenate([k_ref[0, n_full, 0:nk, head_cols(h)], pos_ref[h, 0:nk]], axis=1)
                s_d.append(jnp.dot(lhs, w_ref[h], preferred_element_type=F32))
            scores0(0)
            ms, ds = [], []
            for h in range(nh):
                tail = jnp.where(rel_ref[...] <= 0.0, s_d[h][r * tq:], NEG_INF)
                s_d[h] = tail if r == 0 else jnp.concatenate([s_d[h][:r * tq], tail], axis=0)
                d = slope_ref[h] * float(-r * tq)
                ms.append(jnp.max(s_d[h], axis=0, keepdims=True) + d)
                ds.append(d)
            scores0(1)
            ps = [jnp.exp(s_d[h] - (ms[h] - ds[h])) for h in range(nh)]
            scores0(2)
            out = []
            for h in range(nh):
                vt = jnp.concatenate([vt_ref[0, n_full * sub + u, head_cols(h), :] for u in range(r + 1)], axis=1)
                acc_ref[h] = jnp.dot(vt, ps[h].astype(BF16), preferred_element_type=F32)
                out += [ms[h], jnp.sum(ps[h], axis=0, keepdims=True)]
            scores0(3)
            return tuple(out) + tuple(cm)
        return f

    state = lax.switch(qi % sub, [diagonal(r) for r in range(sub)])

    def step(j, state):
        ml, cmax = state[:2 * nh], state[2 * nh:]
        out, cmax_next = [], []
        for h in range(nh):
            m, l = ml[2 * h], ml[2 * h + 1]
            d = slope_ref[h] * jnp.full((1, 2 * tq), j * tk - qi * tq, jnp.int32).astype(F32)
            m_new = jnp.maximum(m, cmax[h] + d)
            a = jnp.exp(m - m_new)
            shift = m_new - d
            cm, lsum, pv = None, None, None
            for u in range(sub):
                rows = slice(u * tq, (u + 1) * tq)
                lhs = jnp.concatenate([k_ref[0, j + 1, rows, head_cols(h)], pos_ref[h, rows]], axis=1)
                s = jnp.dot(lhs, w_ref[h], preferred_element_type=F32)
                p = jnp.exp(s_ref[h, rows] - shift)
                s_ref[h, rows] = s
                bm = jnp.max(s, axis=0, keepdims=True)
                cm = bm if cm is None else jnp.maximum(cm, bm)
                ps = jnp.sum(p, axis=0, keepdims=True)
                lsum = ps if lsum is None else lsum + ps
                part = jnp.dot(vt_ref[0, j * sub + u, head_cols(h), :], p.astype(BF16),
                               preferred_element_type=F32)
                pv = part if pv is None else pv + part
            acc_ref[h] = a * acc_ref[h] + pv
            out += [m_new, a * l + lsum]
            cmax_next.append(cm)
        return tuple(out) + tuple(cmax_next)

    state = lax.fori_loop(0, n_full - 1, step, state)
    ml, cmax = state[:2 * nh], state[2 * nh:]
    carry = lax.cond(n_full > 0, lambda: absorb(n_full - 1, s_ref, cmax, ml), lambda: ml)

    lam = _diff_lambda(lam_ref, lam_init)
    for h in range(nh):
        on = acc_ref[h] / carry[2 * h + 1]
        ot = on[:, 0:tq] - lam * on[:, tq:]
        ot = ot * lax.rsqrt(jnp.mean(ot * ot, axis=0, keepdims=True) + LN_EPS)
        o_ref[:, h * V_DIM:(h + 1) * V_DIM] = ot.T * g_ref[...] * (1.0 - lam_init)


def _attn_prompt(qt, kb, vt, lam_p, subln_g, *, lam_init):
    batch, nblk, _, tq = qt.shape
    tk, nh = ATTN_TK, HEADS_PER_STEP
    assert tq == ATTN_TQ and (nblk * tq) % tk == 0
    nchunk = nblk * tq // tk
    kb = kb.reshape(batch, nchunk, tk, D_MODEL)
    slopes = np.asarray(_slopes(), np.float32)
    c = np.arange(tk)
    pos = np.zeros((N_HEADS, tk, V_DIM), np.float32)
    pos[:, :, 0] = slopes[:, None] * ((c // 256) * 256)[None]
    pos[:, :, 1] = slopes[:, None] * (c % 256)[None]
    pos = jnp.asarray(pos, BF16)
    rel = (c[:tq, None] - np.tile(np.arange(tq), 2)[None, :]).astype(np.float32)
    slope_row = np.ascontiguousarray(np.broadcast_to(slopes[:, None, None], (N_HEADS, 1, 2 * tq)))
    return pl.pallas_call(
        functools.partial(_attn_prompt_kernel, lam_init=lam_init),
        grid=(batch, N_HEADS // nh, nblk),
        in_specs=[
            pl.BlockSpec((1, 1, nh * V_DIM, tq), lambda b, h, i: (b, i, h, 0)),
            pl.BlockSpec((1, nchunk, tk, nh * V_DIM), lambda b, h, i: (b, 0, 0, h)),
            pl.BlockSpec((1, nblk, nh * V_DIM, tq), lambda b, h, i: (b, 0, h, 0)),
            pl.BlockSpec((nh, tk, V_DIM), lambda b, h, i: (h, 0, 0)),
            pl.BlockSpec(rel.shape, lambda b, h, i: (0, 0), pipeline_mode=pl.Buffered(1)),
            pl.BlockSpec((nh, 1, 2 * tq), lambda b, h, i: (h, 0, 0)),
            pl.BlockSpec(lam_p.shape, lambda b, h, i: (0, 0)),
            pl.BlockSpec(subln_g.shape, lambda b, h, i: (0, 0)),
        ],
        out_specs=pl.BlockSpec((tq, nh * V_DIM), lambda b, h, i: (b * nblk + i, h)),
        out_shape=jax.ShapeDtypeStruct((batch * nblk * tq, D_MODEL), F32),
        scratch_shapes=[pltpu.VMEM((nh, 2 * V_DIM, 2 * tq), BF16), pltpu.VMEM((nh, V_DIM, 2 * tq), F32),
                        pltpu.VMEM((nh, tk, 2 * tq), F32)],
        compiler_params=_params(3),
        name="attn_prompt",
    )(qt, kb, vt, pos, rel, slope_row, lam_p, subln_g)


def _attn_decode_kernel(pt_ref, q_ref, kn_ref, vn_ref, *rest, lam_init, past_len, n_tok):
    g_pages = PAGES_PER_STEP
    k_refs = rest[:g_pages]
    v_refs = rest[g_pages:2 * g_pages]
    bias_ref, biasn_ref, slope_ref, lam_ref, g_ref, o_ref, qm_ref, m_ref, l_ref, acc_ref = rest[2 * g_pages:]
    del pt_ref
    step = pl.program_id(1)
    rows = n_tok * N_HEADS

    @pl.when(step == 0)
    def _():
        q = q_ref[0]
        lane = lax.broadcasted_iota(jnp.int32, q.shape, 1)
        qm_ref[0:rows, :] = jnp.where(lane < HEAD_DIM, q, 0.0).astype(BF16)
        qm_ref[rows:, :] = jnp.where(lane >= HEAD_DIM, q, 0.0).astype(BF16)
        m_ref[...] = jnp.full_like(m_ref, NEG_INF)
        l_ref[...] = jnp.zeros_like(l_ref)
        acc_ref[...] = jnp.zeros_like(acc_ref)

    def update(pages):
        qm = qm_ref[...]
        s_list, mx = [], None
        for kp, _, bias, d in pages:
            s = lax.dot_general(qm, kp, (((1,), (1,)), ((), ())), preferred_element_type=F32) + bias
            s_list.append(s)
            blk = jnp.max(s, axis=1, keepdims=True) + d
            mx = blk if mx is None else jnp.maximum(mx, blk)
        m = m_ref[...]
        m_new = jnp.maximum(m, mx)
        a = jnp.exp(m - m_new)
        l = a * l_ref[...]
        acc = a * acc_ref[...]
        for s, (_, vp, _, d) in zip(s_list, pages):
            p = jnp.exp(s - (m_new - d))
            l = l + jnp.sum(p, axis=1, keepdims=True)
            acc = acc + jnp.dot(p.astype(BF16), vp, preferred_element_type=F32)
        l_ref[...] = l
        acc_ref[...] = acc
        m_ref[...] = m_new

    slope = slope_ref[...]
    pages = []
    for g in range(g_pages):
        page = step * g_pages + g
        d = slope * jnp.full((2 * rows, 1), page * PAGE_SIZE - past_len, jnp.int32).astype(F32)
        pages.append((k_refs[g][0].astype(BF16), v_refs[g][0].astype(BF16), bias_ref[...], d))
    update(pages)

    @pl.when(step == pl.num_programs(1) - 1)
    def _():
        update([(kn_ref[0].astype(BF16), vn_ref[0].astype(BF16), biasn_ref[...], jnp.zeros((2 * rows, 1), F32))])
        lam = _diff_lambda(lam_ref, lam_init)
        on = acc_ref[...] / l_ref[...]
        o = on[0:rows] - lam * on[rows:]
        o = o * lax.rsqrt(jnp.mean(o * o, axis=-1, keepdims=True) + LN_EPS)
        o_ref[0] = o * g_ref[...] * (1.0 - lam_init)


def _attn_decode(q, k_new, v_new, cache_k, cache_v, page_table, lam_p, subln_g, *, lam_init):
    db, rows, _ = q.shape
    n_tok = rows // N_HEADS
    n_pages = page_table.shape[1]
    past_len = n_pages * PAGE_SIZE
    g_pages = PAGES_PER_STEP
    assert n_pages % g_pages == 0
    slopes = np.asarray(_slopes(), np.float32)
    row_h = np.tile(np.arange(N_HEADS), 2 * n_tok)
    row_t = np.tile(np.repeat(np.arange(n_tok), N_HEADS), 2)
    row_slope = slopes[row_h]
    col_pos = np.repeat(np.arange(PAGE_SIZE), N_HEADS)
    col_h = np.tile(np.arange(N_HEADS), PAGE_SIZE)
    bias = np.where(row_h[:, None] == col_h[None, :],
                    row_slope[:, None] * (col_pos[None, :] - row_t[:, None]), NEG_INF).astype(np.float32)
    ncol_t = np.repeat(np.arange(n_tok), N_HEADS)
    ncol_h = np.tile(np.arange(N_HEADS), n_tok)
    dist = row_t[:, None] - ncol_t[None, :]
    bias_new = np.where((row_h[:, None] == ncol_h[None, :]) & (dist >= 0),
                        -row_slope[:, None] * dist, NEG_INF).astype(np.float32)
    slope_col = np.ascontiguousarray(row_slope[:, None])

    tok_spec = pl.BlockSpec((1, rows, V_DIM), lambda b, s, pt: (b, 0, 0))

    def page_spec(g):
        return pl.BlockSpec((1, PAGE_SIZE * N_HEADS, V_DIM), lambda b, s, pt: (pt[b, s * g_pages + g], 0, 0))

    def const2(shape):
        return pl.BlockSpec(shape, lambda b, s, pt: (0, 0))

    grid_spec = pltpu.PrefetchScalarGridSpec(
        num_scalar_prefetch=1,
        grid=(db, n_pages // g_pages),
        in_specs=[tok_spec, tok_spec, tok_spec]
        + [page_spec(g) for g in range(g_pages)] * 2
        + [const2(bias.shape), const2(bias_new.shape), const2(slope_col.shape),
           const2(lam_p.shape), const2(subln_g.shape)],
        out_specs=tok_spec,
        scratch_shapes=[pltpu.VMEM((2 * rows, V_DIM), BF16), pltpu.VMEM((2 * rows, 1), F32),
                        pltpu.VMEM((2 * rows, 1), F32), pltpu.VMEM((2 * rows, V_DIM), F32)],
    )
    return pl.pallas_call(
        functools.partial(_attn_decode_kernel, lam_init=lam_init, past_len=past_len, n_tok=n_tok),
        grid_spec=grid_spec,
        out_shape=jax.ShapeDtypeStruct((db, rows, V_DIM), F32),
        compiler_params=_params(2),
        name="attn_decode",
    )(page_table, q, k_new, v_new, *([cache_k] * g_pages), *([cache_v] * g_pages),
      bias, bias_new, slope_col, lam_p, subln_g)


def _mix_ffn_kernel(h_ref, o_ref, sga_ref, sgm_ref, gu_ref, vn_ref, wsp_ref, bsp_ref, wo_ref,
                    g2_ref, b2_ref, wg_ref, wu_ref, wd_ref, g3_ref, b3_ref, y_ref, s_ref, *, period):
    tm = h_ref.shape[0]
    r = lax.broadcasted_iota(jnp.int32, (CHUNK, CHUNK), 0)
    c = lax.broadcasted_iota(jnp.int32, (CHUNK, CHUNK), 1)
    keep = (r // period == c // period) & (c <= r)
    for grp in range(N_GROUPS):
        w = jnp.where(keep, wsp_ref[grp], 0.0).astype(BF16)
        cols = slice(grp * GROUP_WIDTH, (grp + 1) * GROUP_WIDTH)
        for ch in range(tm // CHUNK):
            rws = slice(ch * CHUNK, (ch + 1) * CHUNK)
            f = jnp.dot(w, vn_ref[rws, cols].astype(BF16), preferred_element_type=F32) + bsp_ref[grp]
            s_ref[rws, cols] = gu_ref[rws, cols] * f
    mix_in = sga_ref[...] * o_ref[...] + sgm_ref[...] * s_ref[...]
    mix = jnp.dot(mix_in.astype(BF16), wo_ref[...], preferred_element_type=F32)
    h2 = _layer_norm(ALPHA * h_ref[...] + mix, g2_ref[...], b2_ref[...])
    y = _swiglu(h2.astype(BF16), wg_ref, wu_ref, wd_ref)
    y_ref[...] = _layer_norm(ALPHA * h2 + 0.5 * y, g3_ref[...], b3_ref[...])


def _mix_ffn(h, o, sga, sgm, gu, vn, wsp, bsp, wo, g2, b2, wg, wu, wd, g3, b3, *, tm, period):
    rows = h.shape[0]
    row_spec = pl.BlockSpec((tm, D_MODEL), lambda i: (i, 0))
    consts = (wsp, bsp, wo, g2, b2, wg, wu, wd, g3, b3)
    return pl.pallas_call(
        functools.partial(_mix_ffn_kernel, period=period),
        grid=(rows // tm,),
        in_specs=[row_spec] * 6 + [_const_spec(a.shape) for a in consts],
        out_specs=row_spec,
        out_shape=jax.ShapeDtypeStruct((rows, D_MODEL), F32),
        scratch_shapes=[pltpu.VMEM((tm, D_MODEL), F32)],
        compiler_params=_params(1),
        name="mix_ffn",
    )(h, o, sga, sgm, gu, vn, *consts)


def _spatial_params(w_spatial, b_spatial, period):
    if period == CHUNK:
        w, b = w_spatial, b_spatial
    else:
        sel = np.equal.outer(np.arange(CHUNK) % period, np.arange(period)).astype(np.float32)
        hi = lax.Precision.HIGHEST
        w = jnp.einsum("ra,gab,cb->grc", sel, w_spatial[:, :period, :period], sel, precision=hi)
        b = jnp.einsum("ra,ga->gr", sel, b_spatial[:, :period], precision=hi)
    return w, b[:, :, None]


def kernel(x_prompt, x_sample, cache_k, cache_v, page_table, ln1_g, ln1_b, ffn1_wg, ffn1_wu, ffn1_wd, w_in, lam_q1, lam_k1, lam_q2, lam_k2, subln_g, gmlp_ln_g, gmlp_ln_b, w_spatial, b_spatial, w_out, ln2_g, ln2_b, ffn2_wg, ffn2_wu, ffn2_wd, ln3_g, ln3_b):
    batch, seq, _ = x_prompt.shape
    db, n_tok, _ = x_sample.shape
    rows_p, rows_s = batch * seq, db * n_tok
    assert rows_s == CHUNK and CHUNK % n_tok == 0 and seq % ATTN_TK == 0
    hp = x_prompt.reshape(rows_p, D_MODEL)
    hs = x_sample.reshape(rows_s, D_MODEL)
    row = lambda a: a.reshape(1, -1)
    tm = ATTN_TQ
    kp_l, vp_l, ks_l, vs_l, gv_l = [], [], [], [], []
    for l in range(DEPTH):
        lam_init = 0.8 - 0.6 * math.exp(-0.3 * l)
        wg1, wu1, wd1 = ffn1_wg[l].astype(BF16), ffn1_wu[l].astype(BF16), ffn1_wd[l].astype(BF16)
        wg2, wu2, wd2 = ffn2_wg[l].astype(BF16), ffn2_wu[l].astype(BF16), ffn2_wd[l].astype(BF16)
        w_in_b, w_out_b = w_in[l].astype(BF16), w_out[l].astype(BF16)
        lam_p = jnp.stack([lam_q1[l], lam_k1[l], lam_q2[l], lam_k2[l]])
        sub_g = row(subln_g[l])
        ffn1 = (wg1, wu1, wd1, row(ln1_g[l]), row(ln1_b[l]))
        tail = (w_out_b, row(ln2_g[l]), row(ln2_b[l]), wg2, wu2, wd2, row(ln3_g[l]), row(ln3_b[l]))
        gln = (row(gmlp_ln_g[l]), row(gmlp_ln_b[l]))

        h1 = _ffn_ln(hp, *ffn1, tm=tm)
        k_p, v_p, qt, kb, vt, gu, vn, sga, sgm = _proj(h1, w_in_b, *gln, tm=tm, batch=batch)
        o = _attn_prompt(qt, kb, vt, lam_p, sub_g, lam_init=lam_init)
        wsp, bsp = _spatial_params(w_spatial[l], b_spatial[l], CHUNK)
        hp = _mix_ffn(h1, o, sga, sgm, gu, vn, wsp, bsp, *tail, tm=tm, period=CHUNK)

        h1 = _ffn_ln(hs, *ffn1, tm=rows_s)
        q_s, k_s, v_s, gu, vn_s, sga, sgm = _proj(h1, w_in_b, *gln, tm=rows_s)
        tok = lambda a: a.reshape(db, n_tok * N_HEADS, V_DIM)
        o = _attn_decode(tok(q_s), tok(k_s), tok(v_s),
                         cache_k[l].reshape(-1, PAGE_SIZE * N_HEADS, V_DIM),
                         cache_v[l].reshape(-1, PAGE_SIZE * N_HEADS, V_DIM),
                         page_table, lam_p, sub_g, lam_init=lam_init)
        wsp, bsp = _spatial_params(w_spatial[l], b_spatial[l], n_tok)
        hs = _mix_ffn(h1, o.reshape(rows_s, D_MODEL), sga, sgm, gu, vn_s, wsp, bsp, *tail, tm=rows_s, period=n_tok)

        kp_l.append(k_p.reshape(batch, seq, N_HEADS, V_DIM))
        vp_l.append(v_p.reshape(batch, seq, N_HEADS, V_DIM))
        ks_l.append(k_s.reshape(db, n_tok, N_HEADS, V_DIM))
        vs_l.append(v_s.reshape(db, n_tok, N_HEADS, V_DIM))
        gv_l.append(vn_s.reshape(db, n_tok, D_MODEL))
    return (hp.reshape(batch, seq, D_MODEL), hs.reshape(db, n_tok, D_MODEL),
            jnp.stack(kp_l), jnp.stack(vp_l), jnp.stack(ks_l), jnp.stack(vs_l), jnp.stack(gv_l))
```

```python
import functools
import math

import jax
import jax.numpy as jnp
import numpy as np
from jax import lax
from jax.experimental import pallas as pl
from jax.experimental.pallas import tpu as pltpu

D_MODEL = 1024
DEPTH = 1
PAGE_SIZE = 128
N_HEADS = 8
HEAD_DIM = 64
V_DIM = 2 * HEAD_DIM
CHUNK = 128
GROUP_WIDTH = 128
N_GROUPS = D_MODEL // GROUP_WIDTH
D_FF = 2816
ALPHA = (2 * DEPTH) ** 0.25
LN_EPS = 1e-5
NEG_INF = -1e30
QK_SCALE = HEAD_DIM ** -0.5
LOG2E = math.log2(math.e)
Q_PRESCALE = QK_SCALE * LOG2E

VMEM_LIMIT_BYTES = 56 * 1024 * 1024
ATTN_TQ = 256
ATTN_TK = 1024
HEADS_PER_STEP = 2
PAGES_PER_STEP = 16

F32 = jnp.float32
BF16 = jnp.bfloat16


def _bf16_terms(x, n):
    terms = []
    for _ in range(n):
        m, e = math.frexp(x)
        t = math.ldexp(round(m * 256) / 256, e)
        terms.append(t)
        x -= t
    return tuple(terms)


LOG2E_BF16_TERMS = _bf16_terms(LOG2E, 4)


def _slopes():
    return [2.0 ** (-8.0 * (h + 1) / N_HEADS) for h in range(N_HEADS)]


def _layer_norm(x, g, b):
    mu = jnp.mean(x, axis=-1, keepdims=True)
    xc = x - mu
    var = jnp.mean(xc * xc, axis=-1, keepdims=True)
    return xc * lax.rsqrt(var + LN_EPS) * g + b


def _gelu(x):
    return 0.5 * x * (1.0 + lax.erf(x * math.sqrt(0.5)))


def _swiglu(xb, wg_ref, wu_ref, wd_ref):
    gate = jnp.dot(xb, wg_ref[...], preferred_element_type=F32)
    up = jnp.dot(xb, wu_ref[...], preferred_element_type=F32)
    act = (gate * jax.nn.sigmoid(gate)) * up
    return jnp.dot(act.astype(BF16), wd_ref[...], preferred_element_type=F32)


def _diff_lambda(lam_ref, lam_init):
    lp = lam_ref[...]
    a = jnp.sum(lp[0:1] * lp[1:2], axis=-1, keepdims=True)
    b = jnp.sum(lp[2:3] * lp[3:4], axis=-1, keepdims=True)
    return jnp.exp(a) - jnp.exp(b) + lam_init


def _const_spec(shape):
    nd = len(shape)
    return pl.BlockSpec(shape, lambda *_: (0,) * nd, pipeline_mode=pl.Buffered(1))


def _params(n_axes):
    return pltpu.CompilerParams(
        dimension_semantics=("arbitrary",) * n_axes, vmem_limit_bytes=VMEM_LIMIT_BYTES)


def _ffn_ln_kernel(x_ref, wg_ref, wu_ref, wd_ref, g_ref, b_ref, o_ref):
    x = x_ref[...]
    y = _swiglu(x.astype(BF16), wg_ref, wu_ref, wd_ref)
    o_ref[...] = _layer_norm(ALPHA * x + 0.5 * y, g_ref[...], b_ref[...])


def _ffn_ln(x, wg, wu, wd, g, b, *, tm):
    rows = x.shape[0]
    row_spec = pl.BlockSpec((tm, D_MODEL), lambda i: (i, 0))
    return pl.pallas_call(
        _ffn_ln_kernel,
        grid=(rows // tm,),
        in_specs=[row_spec, _const_spec(wg.shape), _const_spec(wu.shape), _const_spec(wd.shape),
                  _const_spec(g.shape), _const_spec(b.shape)],
        out_specs=row_spec,
        out_shape=jax.ShapeDtypeStruct((rows, D_MODEL), F32),
        compiler_params=_params(1),
        name="ffn_ln",
    )(x, wg, wu, wd, g, b)


def _proj_kernel(h_ref, w_ref, lg_ref, lb_ref, *out_refs, attn_layouts):
    hb = h_ref[...].astype(BF16)

    def section(i):
        return jnp.dot(hb, w_ref[:, i * D_MODEL:(i + 1) * D_MODEL], preferred_element_type=F32)

    q = section(0) * Q_PRESCALE
    k = section(1)
    v = section(2)
    if attn_layouts:
        k_ref, v_ref, qt_ref, kb_ref, vt_ref, gu_ref, vn_ref, sga_ref, sgm_ref = out_refs
        qt_ref[0, 0] = q.T.astype(BF16)
        kb_ref[...] = k.astype(BF16)
        vt_ref[0, 0] = v.T.astype(BF16)
    else:
        q_ref, k_ref, v_ref, gu_ref, vn_ref, sga_ref, sgm_ref = out_refs
        q_ref[...] = q
    k_ref[...] = k
    v_ref[...] = v
    gu_ref[...] = _gelu(section(3))
    vn_ref[...] = _layer_norm(_gelu(section(4)), lg_ref[...], lb_ref[...])
    sga_ref[...] = jax.nn.sigmoid(section(5))
    sgm_ref[...] = jax.nn.sigmoid(section(6))


def _proj(h, w_in, ln_g, ln_b, *, tm, batch=None):
    rows = h.shape[0]
    n_tiles = rows // tm
    row_spec = pl.BlockSpec((tm, D_MODEL), lambda i: (i, 0))
    row_f32 = jax.ShapeDtypeStruct((rows, D_MODEL), F32)
    attn_layouts = batch is not None
    if attn_layouts:
        per_b = n_tiles // batch
        t_spec = pl.BlockSpec((1, 1, D_MODEL, tm), lambda i: (i // per_b, i % per_b, 0, 0))
        t_shape = jax.ShapeDtypeStruct((batch, per_b, D_MODEL, tm), BF16)
        out_specs = [row_spec, row_spec, t_spec, row_spec, t_spec] + [row_spec] * 4
        out_shape = [row_f32, row_f32, t_shape, jax.ShapeDtypeStruct((rows, D_MODEL), BF16), t_shape] + [row_f32] * 4
    else:
        out_specs = [row_spec] * 7
        out_shape = [row_f32] * 7
    return pl.pallas_call(
        functools.partial(_proj_kernel, attn_layouts=attn_layouts),
        grid=(n_tiles,),
        in_specs=[row_spec, _const_spec(w_in.shape), _const_spec(ln_g.shape), _const_spec(ln_b.shape)],
        out_specs=out_specs,
        out_shape=out_shape,
        compiler_params=_params(1),
        name="proj",
    )(h, w_in, ln_g, ln_b)


def _attn_prompt_kernel(qt_ref, k_ref, vt_ref, pos_ref, rel_ref, slope_ref, lam_ref, g_ref, o_ref,
                        w_ref, acc_ref, s_ref, *, lam_init):
    tq, tk, nh = ATTN_TQ, ATTN_TK, HEADS_PER_STEP
    sub = tk // tq
    qi = pl.program_id(2)
    n_full = qi // sub
    zeros = jnp.zeros((HEAD_DIM, tq), BF16)
    prow = lax.broadcasted_iota(jnp.int32, (V_DIM, 2 * tq), 0)
    pos_rows = jnp.zeros((V_DIM, 2 * tq), F32)
    for i, term in enumerate(LOG2E_BF16_TERMS * 2):
        pos_rows = jnp.where(prow == i, term, pos_rows)
    pos_rows = pos_rows.astype(BF16)
    for h in range(nh):
        qt = qt_ref[0, 0, h * V_DIM:(h + 1) * V_DIM, :]
        w_ref[h, 0:HEAD_DIM, 0:tq] = qt[0:HEAD_DIM]
        w_ref[h, HEAD_DIM:V_DIM, 0:tq] = zeros
        w_ref[h, 0:HEAD_DIM, tq:] = zeros
        w_ref[h, HEAD_DIM:V_DIM, tq:] = qt[HEAD_DIM:]
        w_ref[h, V_DIM:, :] = pos_rows

    def head_cols(h):
        return slice(h * V_DIM, (h + 1) * V_DIM)

    def absorb(j, src_ref, cmax, ml):
        out = []
        for h in range(nh):
            m, l = ml[2 * h], ml[2 * h + 1]
            d = slope_ref[h] * jnp.full((1, 2 * tq), j * tk - qi * tq, jnp.int32).astype(F32)
            m_new = jnp.maximum(m, cmax[h] + d)
            a = jnp.exp2(m - m_new)
            p = jnp.exp2(src_ref[h] - (m_new - d))
            vt = jnp.concatenate([vt_ref[0, j * sub + u, head_cols(h), :] for u in range(sub)], axis=1)
            acc_ref[h] = a * acc_ref[h] + jnp.dot(vt, p.astype(BF16), preferred_element_type=F32)
            out += [m_new, a * l + jnp.sum(p, axis=0, keepdims=True)]
        return tuple(out)

    def diagonal(r):
        def f():
            nk = (r + 1) * tq
            cm = [None] * nh

            def scores0(u):
                rows = slice(u * tq, (u + 1) * tq)
                for h in range(nh):
                    lhs = jnp.concatenate([k_ref[0, 0, rows, head_cols(h)], pos_ref[h, rows]], axis=1)
                    s0 = jnp.dot(lhs, w_ref[h], preferred_element_type=F32)
                    s_ref[h, rows] = s0
                    bm = jnp.max(s0, axis=0, keepdims=True)
                    cm[h] = bm if cm[h] is None else jnp.maximum(cm[h], bm)

            s_d = []
            for h in range(nh):
                lhs = jnp.concatenate([k_ref[0, n_full, 0:nk, head_cols(h)], pos_ref[h, 0:nk]], axis=1)
                s_d.append(jnp.dot(lhs, w_ref[h], preferred_element_type=F32))
            scores0(0)
            ms, ds = [], []
            for h in range(nh):
                tail = jnp.where(rel_ref[...] <= 0.0, s_d[h][r * tq:], NEG_INF)
                s_d[h] = tail if r == 0 else jnp.concatenate([s_d[h][:r * tq], tail], axis=0)
                d = slope_ref[h] * float(-r * tq)
                ms.append(jnp.max(s_d[h], axis=0, keepdims=True) + d)
                ds.append(d)
            scores0(1)
            ps = [jnp.exp2(s_d[h] - (ms[h] - ds[h])) for h in range(nh)]
            scores0(2)
            out = []
            for h in range(nh):
                vt = jnp.concatenate([vt_ref[0, n_full * sub + u, head_cols(h), :] for u in range(r + 1)], axis=1)
                acc_ref[h] = jnp.dot(vt, ps[h].astype(BF16), preferred_element_type=F32)
                out += [ms[h], jnp.sum(ps[h], axis=0, keepdims=True)]
            scores0(3)
            return tuple(out) + tuple(cm)
        return f

    state = lax.switch(qi % sub, [diagonal(r) for r in range(sub)])

    def step(j, state):
        ml, cmax = state[:2 * nh], state[2 * nh:]
        out, cmax_next = [], []
        for h in range(nh):
            m, l = ml[2 * h], ml[2 * h + 1]
            d = slope_ref[h] * jnp.full((1, 2 * tq), j * tk - qi * tq, jnp.int32).astype(F32)
            m_new = jnp.maximum(m, cmax[h] + d)
            a = jnp.exp2(m - m_new)
            shift = m_new - d
            cm, lsum, pv = None, None, None
            for u in range(sub):
                rows = slice(u * tq, (u + 1) * tq)
                lhs = jnp.concatenate([k_ref[0, j + 1, rows, head_cols(h)], pos_ref[h, rows]], axis=1)
                s = jnp.dot(lhs, w_ref[h], preferred_element_type=F32)
                p = jnp.exp2(s_ref[h, rows] - shift)
                s_ref[h, rows] = s
                bm = jnp.max(s, axis=0, keepdims=True)
                cm = bm if cm is None else jnp.maximum(cm, bm)
                ps = jnp.sum(p, axis=0, keepdims=True)
                lsum = ps if lsum is None else lsum + ps
                part = jnp.dot(vt_ref[0, j * sub + u, head_cols(h), :], p.astype(BF16),
                               preferred_element_type=F32)
                pv = part if pv is None else pv + part
            acc_ref[h] = a * acc_ref[h] + pv
            out += [m_new, a * l + lsum]
            cmax_next.append(cm)
        return tuple(out) + tuple(cmax_next)

    state = lax.fori_loop(0, n_full - 1, step, state)
    ml, cmax = state[:2 * nh], state[2 * nh:]
    carry = lax.cond(n_full > 0, lambda: absorb(n_full - 1, s_ref, cmax, ml), lambda: ml)

    lam = _diff_lambda(lam_ref, lam_init)
    for h in range(nh):
        on = acc_ref[h] / carry[2 * h + 1]
        ot = on[:, 0:tq] - lam * on[:, tq:]
        ot = ot * lax.rsqrt(jnp.mean(ot * ot, axis=0, keepdims=True) + LN_EPS)
        o_ref[:, h * V_DIM:(h + 1) * V_DIM] = ot.T * g_ref[...] * (1.0 - lam_init)


def _attn_prompt(qt, kb, vt, lam_p, subln_g, *, lam_init):
    batch, nblk, _, tq = qt.shape
    tk, nh = ATTN_TK, HEADS_PER_STEP
    assert tq == ATTN_TQ and (nblk * tq) % tk == 0
    nchunk = nblk * tq // tk
    kb = kb.reshape(batch, nchunk, tk, D_MODEL)
    slopes = np.asarray(_slopes(), np.float32)
    c = np.arange(tk)
    pos = np.zeros((N_HEADS, tk, V_DIM), np.float32)
    n_terms = len(LOG2E_BF16_TERMS)
    for i in range(n_terms):
        pos[:, :, i] = slopes[:, None] * ((c // 256) * 256)[None]
        pos[:, :, n_terms + i] = slopes[:, None] * (c % 256)[None]
    pos = jnp.asarray(pos, BF16)
    rel = (c[:tq, None] - np.tile(np.arange(tq), 2)[None, :]).astype(np.float32)
    slope_row = np.ascontiguousarray(np.broadcast_to((slopes * LOG2E)[:, None, None], (N_HEADS, 1, 2 * tq)))
    return pl.pallas_call(
        functools.partial(_attn_prompt_kernel, lam_init=lam_init),
        grid=(batch, N_HEADS // nh, nblk),
        in_specs=[
            pl.BlockSpec((1, 1, nh * V_DIM, tq), lambda b, h, i: (b, i, h, 0)),
            pl.BlockSpec((1, nchunk, tk, nh * V_DIM), lambda b, h, i: (b, 0, 0, h)),
            pl.BlockSpec((1, nblk, nh * V_DIM, tq), lambda b, h, i: (b, 0, h, 0)),
            pl.BlockSpec((nh, tk, V_DIM), lambda b, h, i: (h, 0, 0)),
            pl.BlockSpec(rel.shape, lambda b, h, i: (0, 0), pipeline_mode=pl.Buffered(1)),
            pl.BlockSpec((nh, 1, 2 * tq), lambda b, h, i: (h, 0, 0)),
            pl.BlockSpec(lam_p.shape, lambda b, h, i: (0, 0)),
            pl.BlockSpec(subln_g.shape, lambda b, h, i: (0, 0)),
        ],
        out_specs=pl.BlockSpec((tq, nh * V_DIM), lambda b, h, i: (b * nblk + i, h)),
        out_shape=jax.ShapeDtypeStruct((batch * nblk * tq, D_MODEL), F32),
        scratch_shapes=[pltpu.VMEM((nh, 2 * V_DIM, 2 * tq), BF16), pltpu.VMEM((nh, V_DIM, 2 * tq), F32),
                        pltpu.VMEM((nh, tk, 2 * tq), F32)],
        compiler_params=_params(3),
        name="attn_prompt",
    )(qt, kb, vt, pos, rel, slope_row, lam_p, subln_g)


def _attn_decode_kernel(pt_ref, q_ref, kn_ref, vn_ref, *rest, lam_init, past_len, n_tok):
    g_pages = PAGES_PER_STEP
    k_refs = rest[:g_pages]
    v_refs = rest[g_pages:2 * g_pages]
    bias_ref, biasn_ref, slope_ref, lam_ref, g_ref, o_ref, qm_ref, m_ref, l_ref, acc_ref = rest[2 * g_pages:]
    del pt_ref
    step = pl.program_id(1)
    rows = n_tok * N_HEADS

    @pl.when(step == 0)
    def _():
        q = q_ref[0]
        lane = lax.broadcasted_iota(jnp.int32, q.shape, 1)
        qm_ref[0:rows, :] = jnp.where(lane < HEAD_DIM, q, 0.0).astype(BF16)
        qm_ref[rows:, :] = jnp.where(lane >= HEAD_DIM, q, 0.0).astype(BF16)
        m_ref[...] = jnp.full_like(m_ref, NEG_INF)
        l_ref[...] = jnp.zeros_like(l_ref)
        acc_ref[...] = jnp.zeros_like(acc_ref)

    def update(pages):
        qm = qm_ref[...]
        s_list, mx = [], None
        for kp, _, bias, d in pages:
            s = lax.dot_general(qm, kp, (((1,), (1,)), ((), ())), preferred_element_type=F32) + bias
            s_list.append(s)
            blk = jnp.max(s, axis=1, keepdims=True) + d
            mx = blk if mx is None else jnp.maximum(mx, blk)
        m = m_ref[...]
        m_new = jnp.maximum(m, mx)
        a = jnp.exp2(m - m_new)
        l = a * l_ref[...]
        acc = a * acc_ref[...]
        for s, (_, vp, _, d) in zip(s_list, pages):
            p = jnp.exp2(s - (m_new - d))
            l = l + jnp.sum(p, axis=1, keepdims=True)
            acc = acc + jnp.dot(p.astype(BF16), vp, preferred_element_type=F32)
        l_ref[...] = l
        acc_ref[...] = acc
        m_ref[...] = m_new

    slope = slope_ref[...]
    pages = []
    for g in range(g_pages):
        page = step * g_pages + g
        d = slope * jnp.full((2 * rows, 1), page * PAGE_SIZE - past_len, jnp.int32).astype(F32)
        pages.append((k_refs[g][0].astype(BF16), v_refs[g][0].astype(BF16), bias_ref[...], d))
    update(pages)

    @pl.when(step == pl.num_programs(1) - 1)
    def _():
        update([(kn_ref[0].astype(BF16), vn_ref[0].astype(BF16), biasn_ref[...], jnp.zeros((2 * rows, 1), F32))])
        lam = _diff_lambda(lam_ref, lam_init)
        on = acc_ref[...] / l_ref[...]
        o = on[0:rows] - lam * on[rows:]
        o = o * lax.rsqrt(jnp.mean(o * o, axis=-1, keepdims=True) + LN_EPS)
        o_ref[0] = o * g_ref[...] * (1.0 - lam_init)


def _attn_decode(q, k_new, v_new, cache_k, cache_v, page_table, lam_p, subln_g, *, lam_init):
    db, rows, _ = q.shape
    n_tok = rows // N_HEADS
    n_pages = page_table.shape[1]
    past_len = n_pages * PAGE_SIZE
    g_pages = PAGES_PER_STEP
    assert n_pages % g_pages == 0
    slopes = np.asarray(_slopes(), np.float32)
    row_h = np.tile(np.arange(N_HEADS), 2 * n_tok)
    row_t = np.tile(np.repeat(np.arange(n_tok), N_HEADS), 2)
    row_slope = (slopes * LOG2E)[row_h]
    col_pos = np.repeat(np.arange(PAGE_SIZE), N_HEADS)
    col_h = np.tile(np.arange(N_HEADS), PAGE_SIZE)
    bias = np.where(row_h[:, None] == col_h[None, :],
                    row_slope[:, None] * (col_pos[None, :] - row_t[:, None]), NEG_INF).astype(np.float32)
    ncol_t = np.repeat(np.arange(n_tok), N_HEADS)
    ncol_h = np.tile(np.arange(N_HEADS), n_tok)
    dist = row_t[:, None] - ncol_t[None, :]
    bias_new = np.where((row_h[:, None] == ncol_h[None, :]) & (dist >= 0),
                        -row_slope[:, None] * dist, NEG_INF).astype(np.float32)
    slope_col = np.ascontiguousarray(row_slope[:, None])

    tok_spec = pl.BlockSpec((1, rows, V_DIM), lambda b, s, pt: (b, 0, 0))

    def page_spec(g):
        return pl.BlockSpec((1, PAGE_SIZE * N_HEADS, V_DIM), lambda b, s, pt: (pt[b, s * g_pages + g], 0, 0))

    def const2(shape):
        return pl.BlockSpec(shape, lambda b, s, pt: (0, 0))

    grid_spec = pltpu.PrefetchScalarGridSpec(
        num_scalar_prefetch=1,
        grid=(db, n_pages // g_pages),
        in_specs=[tok_spec, tok_spec, tok_spec]
        + [page_spec(g) for g in range(g_pages)] * 2
        + [const2(bias.shape), const2(bias_new.shape), const2(slope_col.shape),
           const2(lam_p.shape), const2(subln_g.shape)],
        out_specs=tok_spec,
        scratch_shapes=[pltpu.VMEM((2 * rows, V_DIM), BF16), pltpu.VMEM((2 * rows, 1), F32),
                        pltpu.VMEM((2 * rows, 1), F32), pltpu.VMEM((2 * rows, V_DIM), F32)],
    )
    return pl.pallas_call(
        functools.partial(_attn_decode_kernel, lam_init=lam_init, past_len=past_len, n_tok=n_tok),
        grid_spec=grid_spec,
        out_shape=jax.ShapeDtypeStruct((db, rows, V_DIM), F32),
        compiler_params=_params(2),
        name="attn_decode",
    )(page_table, q, k_new, v_new, *([cache_k] * g_pages), *([cache_v] * g_pages),
      bias, bias_new, slope_col, lam_p, subln_g)


def _mix_ffn_kernel(h_ref, o_ref, sga_ref, sgm_ref, gu_ref, vn_ref, wsp_ref, bsp_ref, wo_ref,
                    g2_ref, b2_ref, wg_ref, wu_ref, wd_ref, g3_ref, b3_ref, y_ref, s_ref, *, period):
    tm = h_ref.shape[0]
    r = lax.broadcasted_iota(jnp.int32, (CHUNK, CHUNK), 0)
    c = lax.broadcasted_iota(jnp.int32, (CHUNK, CHUNK), 1)
    keep = (r // period == c // period) & (c <= r)
    for grp in range(N_GROUPS):
        w = jnp.where(keep, wsp_ref[grp], 0.0).astype(BF16)
        cols = slice(grp * GROUP_WIDTH, (grp + 1) * GROUP_WIDTH)
        for ch in range(tm // CHUNK):
            rws = slice(ch * CHUNK, (ch + 1) * CHUNK)
            f = jnp.dot(w, vn_ref[rws, cols].astype(BF16), preferred_element_type=F32) + bsp_ref[grp]
            s_ref[rws, cols] = gu_ref[rws, cols] * f
    mix_in = sga_ref[...] * o_ref[...] + sgm_ref[...] * s_ref[...]
    mix = jnp.dot(mix_in.astype(BF16), wo_ref[...], preferred_element_type=F32)
    h2 = _layer_norm(ALPHA * h_ref[...] + mix, g2_ref[...], b2_ref[...])
    y = _swiglu(h2.astype(BF16), wg_ref, wu_ref, wd_ref)
    y_ref[...] = _layer_norm(ALPHA * h2 + 0.5 * y, g3_ref[...], b3_ref[...])


def _mix_ffn(h, o, sga, sgm, gu, vn, wsp, bsp, wo, g2, b2, wg, wu, wd, g3, b3, *, tm, period):
    rows = h.shape[0]
    row_spec = pl.BlockSpec((tm, D_MODEL), lambda i: (i, 0))
    consts = (wsp, bsp, wo, g2, b2, wg, wu, wd, g3, b3)
    return pl.pallas_call(
        functools.partial(_mix_ffn_kernel, period=period),
        grid=(rows // tm,),
        in_specs=[row_spec] * 6 + [_const_spec(a.shape) for a in consts],
        out_specs=row_spec,
        out_shape=jax.ShapeDtypeStruct((rows, D_MODEL), F32),
        scratch_shapes=[pltpu.VMEM((tm, D_MODEL), F32)],
        compiler_params=_params(1),
        name="mix_ffn",
    )(h, o, sga, sgm, gu, vn, *consts)


def _spatial_params(w_spatial, b_spatial, period):
    if period == CHUNK:
        w, b = w_spatial, b_spatial
    else:
        sel = np.equal.outer(np.arange(CHUNK) % period, np.arange(period)).astype(np.float32)
        hi = lax.Precision.HIGHEST
        w = jnp.einsum("ra,gab,cb->grc", sel, w_spatial[:, :period, :period], sel, precision=hi)
        b = jnp.einsum("ra,ga->gr", sel, b_spatial[:, :period], precision=hi)
    return w, b[:, :, None]


def kernel(x_prompt, x_sample, cache_k, cache_v, page_table, ln1_g, ln1_b, ffn1_wg, ffn1_wu, ffn1_wd, w_in, lam_q1, lam_k1, lam_q2, lam_k2, subln_g, gmlp_ln_g, gmlp_ln_b, w_spatial, b_spatial, w_out, ln2_g, ln2_b, ffn2_wg, ffn2_wu, ffn2_wd, ln3_g, ln3_b):
    batch, seq, _ = x_prompt.shape
    db, n_tok, _ = x_sample.shape
    rows_p, rows_s = batch * seq, db * n_tok
    assert rows_s == CHUNK and CHUNK % n_tok == 0 and seq % ATTN_TK == 0
    hp = x_prompt.reshape(rows_p, D_MODEL)
    hs = x_sample.reshape(rows_s, D_MODEL)
    row = lambda a: a.reshape(1, -1)
    tm = ATTN_TQ
    kp_l, vp_l, ks_l, vs_l, gv_l = [], [], [], [], []
    for l in range(DEPTH):
        lam_init = 0.8 - 0.6 * math.exp(-0.3 * l)
        wg1, wu1, wd1 = ffn1_wg[l].astype(BF16), ffn1_wu[l].astype(BF16), ffn1_wd[l].astype(BF16)
        wg2, wu2, wd2 = ffn2_wg[l].astype(BF16), ffn2_wu[l].astype(BF16), ffn2_wd[l].astype(BF16)
        w_in_b, w_out_b = w_in[l].astype(BF16), w_out[l].astype(BF16)
        lam_p = jnp.stack([lam_q1[l], lam_k1[l], lam_q2[l], lam_k2[l]])
        sub_g = row(subln_g[l])
        ffn1 = (wg1, wu1, wd1, row(ln1_g[l]), row(ln1_b[l]))
        tail = (w_out_b, row(ln2_g[l]), row(ln2_b[l]), wg2, wu2, wd2, row(ln3_g[l]), row(ln3_b[l]))
        gln = (row(gmlp_ln_g[l]), row(gmlp_ln_b[l]))

        h1 = _ffn_ln(hp, *ffn1, tm=tm)
        k_p, v_p, qt, kb, vt, gu, vn, sga, sgm = _proj(h1, w_in_b, *gln, tm=tm, batch=batch)
        o = _attn_prompt(qt, kb, vt, lam_p, sub_g, lam_init=lam_init)
        wsp, bsp = _spatial_params(w_spatial[l], b_spatial[l], CHUNK)
        hp = _mix_ffn(h1, o, sga, sgm, gu, vn, wsp, bsp, *tail, tm=tm, period=CHUNK)

        h1 = _ffn_ln(hs, *ffn1, tm=rows_s)
        q_s, k_s, v_s, gu, vn_s, sga, sgm = _proj(h1, w_in_b, *gln, tm=rows_s)
        tok = lambda a: a.reshape(db, n_tok * N_HEADS, V_DIM)
        o = _attn_decode(tok(q_s), tok(k_s), tok(v_s),
                         cache_k[l].reshape(-1, PAGE_SIZE * N_HEADS, V_DIM),
                         cache_v[l].reshape(-1, PAGE_SIZE * N_HEADS, V_DIM),
                         page_table, lam_p, sub_g, lam_init=lam_init)
        wsp, bsp = _spatial_params(w_spatial[l], b_spatial[l], n_tok)
        hs = _mix_ffn(h1, o.reshape(rows_s, D_MODEL), sga, sgm, gu, vn_s, wsp, bsp, *tail, tm=rows_s, period=n_tok)

        kp_l.append(k_p.reshape(batch, seq, N_HEADS, V_DIM))
        vp_l.append(v_p.reshape(batch, seq, N_HEADS, V_DIM))
        ks_l.append(k_s.reshape(db, n_tok, N_HEADS, V_DIM))
        vs_l.append(v_s.reshape(db, n_tok, N_HEADS, V_DIM))
        gv_l.append(vn_s.reshape(db, n_tok, D_MODEL))
    return (hp.reshape(batch, seq, D_MODEL), hs.reshape(db, n_tok, D_MODEL),
            jnp.stack(kp_l), jnp.stack(vp_l), jnp.stack(ks_l), jnp.stack(vs_l), jnp.stack(gv_l))
```

```python
import functools
import math

import jax
import jax.numpy as jnp
import numpy as np
from jax import lax
from jax.experimental import pallas as pl
from jax.experimental.pallas import tpu as pltpu

D_MODEL = 1024
DEPTH = 1
PAGE_SIZE = 128
N_HEADS = 8
HEAD_DIM = 64
V_DIM = 2 * HEAD_DIM
CHUNK = 128
GROUP_WIDTH = 128
N_GROUPS = D_MODEL // GROUP_WIDTH
D_FF = 2816
ALPHA = (2 * DEPTH) ** 0.25
LN_EPS = 1e-5
NEG_INF = -1e30
QK_SCALE = HEAD_DIM ** -0.5
LOG2E = math.log2(math.e)
Q_PRESCALE = QK_SCALE * LOG2E

VMEM_LIMIT_BYTES = 56 * 1024 * 1024
ATTN_TQ = 256
ATTN_TK = 1024
HEADS_PER_STEP = 2
PAGES_PER_STEP = 8
PAGE_SLOTS = 3

F32 = jnp.float32
BF16 = jnp.bfloat16


def _bf16_terms(x, n):
    terms = []
    for _ in range(n):
        m, e = math.frexp(x)
        t = math.ldexp(round(m * 256) / 256, e)
        terms.append(t)
        x -= t
    return tuple(terms)


LOG2E_BF16_TERMS = _bf16_terms(LOG2E, 4)


def _slopes():
    return [2.0 ** (-8.0 * (h + 1) / N_HEADS) for h in range(N_HEADS)]


def _layer_norm(x, g, b):
    mu = jnp.mean(x, axis=-1, keepdims=True)
    xc = x - mu
    var = jnp.mean(xc * xc, axis=-1, keepdims=True)
    return xc * lax.rsqrt(var + LN_EPS) * g + b


def _gelu(x):
    return 0.5 * x * (1.0 + lax.erf(x * math.sqrt(0.5)))


def _swiglu(xb, wg_ref, wu_ref, wd_ref):
    gate = jnp.dot(xb, wg_ref[...], preferred_element_type=F32)
    up = jnp.dot(xb, wu_ref[...], preferred_element_type=F32)
    act = (gate * jax.nn.sigmoid(gate)) * up
    return jnp.dot(act.astype(BF16), wd_ref[...], preferred_element_type=F32)


def _diff_lambda(lam_ref, lam_init):
    lp = lam_ref[...]
    a = jnp.sum(lp[0:1] * lp[1:2], axis=-1, keepdims=True)
    b = jnp.sum(lp[2:3] * lp[3:4], axis=-1, keepdims=True)
    return jnp.exp(a) - jnp.exp(b) + lam_init


def _const_spec(shape):
    nd = len(shape)
    return pl.BlockSpec(shape, lambda *_: (0,) * nd, pipeline_mode=pl.Buffered(1))


def _params(n_axes):
    return pltpu.CompilerParams(
        dimension_semantics=("arbitrary",) * n_axes, vmem_limit_bytes=VMEM_LIMIT_BYTES)


def _ffn_ln_kernel(x_ref, wg_ref, wu_ref, wd_ref, g_ref, b_ref, o_ref):
    x = x_ref[...]
    y = _swiglu(x.astype(BF16), wg_ref, wu_ref, wd_ref)
    o_ref[...] = _layer_norm(ALPHA * x + 0.5 * y, g_ref[...], b_ref[...])


def _ffn_ln(x, wg, wu, wd, g, b, *, tm):
    rows = x.shape[0]
    row_spec = pl.BlockSpec((tm, D_MODEL), lambda i: (i, 0))
    return pl.pallas_call(
        _ffn_ln_kernel,
        grid=(rows // tm,),
        in_specs=[row_spec, _const_spec(wg.shape), _const_spec(wu.shape), _const_spec(wd.shape),
                  _const_spec(g.shape), _const_spec(b.shape)],
        out_specs=row_spec,
        out_shape=jax.ShapeDtypeStruct((rows, D_MODEL), F32),
        compiler_params=_params(1),
        name="ffn_ln",
    )(x, wg, wu, wd, g, b)


def _proj_kernel(h_ref, w_ref, lg_ref, lb_ref, *out_refs, attn_layouts):
    hb = h_ref[...].astype(BF16)

    def section(i):
        return jnp.dot(hb, w_ref[:, i * D_MODEL:(i + 1) * D_MODEL], preferred_element_type=F32)

    q = section(0) * Q_PRESCALE
    k = section(1)
    v = section(2)
    if attn_layouts:
        k_ref, v_ref, qt_ref, kb_ref, vt_ref, gu_ref, vn_ref, sga_ref, sgm_ref = out_refs
        qt_ref[0, 0] = q.T.astype(BF16)
        kb_ref[...] = k.astype(BF16)
        vt_ref[0, 0] = v.T.astype(BF16)
    else:
        q_ref, k_ref, v_ref, gu_ref, vn_ref, sga_ref, sgm_ref = out_refs
        q_ref[...] = q
    k_ref[...] = k
    v_ref[...] = v
    gu_ref[...] = _gelu(section(3))
    vn_ref[...] = _layer_norm(_gelu(section(4)), lg_ref[...], lb_ref[...])
    sga_ref[...] = jax.nn.sigmoid(section(5))
    sgm_ref[...] = jax.nn.sigmoid(section(6))


def _proj(h, w_in, ln_g, ln_b, *, tm, batch=None):
    rows = h.shape[0]
    n_tiles = rows // tm
    row_spec = pl.BlockSpec((tm, D_MODEL), lambda i: (i, 0))
    row_f32 = jax.ShapeDtypeStruct((rows, D_MODEL), F32)
    attn_layouts = batch is not None
    if attn_layouts:
        per_b = n_tiles // batch
        t_spec = pl.BlockSpec((1, 1, D_MODEL, tm), lambda i: (i // per_b, i % per_b, 0, 0))
        t_shape = jax.ShapeDtypeStruct((batch, per_b, D_MODEL, tm), BF16)
        out_specs = [row_spec, row_spec, t_spec, row_spec, t_spec] + [row_spec] * 4
        out_shape = [row_f32, row_f32, t_shape, jax.ShapeDtypeStruct((rows, D_MODEL), BF16), t_shape] + [row_f32] * 4
    else:
        out_specs = [row_spec] * 7
        out_shape = [row_f32] * 7
    return pl.pallas_call(
        functools.partial(_proj_kernel, attn_layouts=attn_layouts),
        grid=(n_tiles,),
        in_specs=[row_spec, _const_spec(w_in.shape), _const_spec(ln_g.shape), _const_spec(ln_b.shape)],
        out_specs=out_specs,
        out_shape=out_shape,
        compiler_params=_params(1),
        name="proj",
    )(h, w_in, ln_g, ln_b)


def _attn_prompt_kernel(qt_ref, k_ref, vt_ref, pos_ref, rel_ref, slope_ref, lam_ref, g_ref, o_ref,
                        w_ref, acc_ref, s_ref, *, lam_init):
    tq, tk, nh = ATTN_TQ, ATTN_TK, HEADS_PER_STEP
    sub = tk // tq
    qi = pl.program_id(2)
    n_full = qi // sub
    zeros = jnp.zeros((HEAD_DIM, tq), BF16)
    prow = lax.broadcasted_iota(jnp.int32, (V_DIM, 2 * tq), 0)
    pos_rows = jnp.zeros((V_DIM, 2 * tq), F32)
    for i, term in enumerate(LOG2E_BF16_TERMS * 2):
        pos_rows = jnp.where(prow == i, term, pos_rows)
    pos_rows = pos_rows.astype(BF16)
    for h in range(nh):
        qt = qt_ref[0, 0, h * V_DIM:(h + 1) * V_DIM, :]
        w_ref[h, 0:HEAD_DIM, 0:tq] = qt[0:HEAD_DIM]
        w_ref[h, HEAD_DIM:V_DIM, 0:tq] = zeros
        w_ref[h, 0:HEAD_DIM, tq:] = zeros
        w_ref[h, HEAD_DIM:V_DIM, tq:] = qt[HEAD_DIM:]
        w_ref[h, V_DIM:, :] = pos_rows

    def head_cols(h):
        return slice(h * V_DIM, (h + 1) * V_DIM)

    def absorb(j, src_ref, cmax, ml):
        out = []
        for h in range(nh):
            m, l = ml[2 * h], ml[2 * h + 1]
            d = slope_ref[h] * jnp.full((1, 2 * tq), j * tk - qi * tq, jnp.int32).astype(F32)
            m_new = jnp.maximum(m, cmax[h] + d)
            a = jnp.exp2(m - m_new)
            p = jnp.exp2(src_ref[h] - (m_new - d))
            vt = jnp.concatenate([vt_ref[0, j * sub + u, head_cols(h), :] for u in range(sub)], axis=1)
            acc_ref[h] = a * acc_ref[h] + jnp.dot(vt, p.astype(BF16), preferred_element_type=F32)
            out += [m_new, a * l + jnp.sum(p, axis=0, keepdims=True)]
        return tuple(out)

    def diagonal(r):
        def f():
            nk = (r + 1) * tq
            cm = [None] * nh

            def scores0(u):
                rows = slice(u * tq, (u + 1) * tq)
                for h in range(nh):
                    lhs = jnp.concatenate([k_ref[0, 0, rows, head_cols(h)], pos_ref[h, rows]], axis=1)
                    s0 = jnp.dot(lhs, w_ref[h], preferred_element_type=F32)
                    s_ref[h, rows] = s0
                    bm = jnp.max(s0, axis=0, keepdims=True)
                    cm[h] = bm if cm[h] is None else jnp.maximum(cm[h], bm)

            s_d = []
            for h in range(nh):
                lhs = jnp.concatenate([k_ref[0, n_full, 0:nk, head_cols(h)], pos_ref[h, 0:nk]], axis=1)
                s_d.append(jnp.dot(lhs, w_ref[h], preferred_element_type=F32))
            scores0(0)
            ms, ds = [], []
            for h in range(nh):
                tail = jnp.where(rel_ref[...] <= 0.0, s_d[h][r * tq:], NEG_INF)
                s_d[h] = tail if r == 0 else jnp.concatenate([s_d[h][:r * tq], tail], axis=0)
                d = slope_ref[h] * float(-r * tq)
                ms.append(jnp.max(s_d[h], axis=0, keepdims=True) + d)
                ds.append(d)
            scores0(1)
            ps = [jnp.exp2(s_d[h] - (ms[h] - ds[h])) for h in range(nh)]
            scores0(2)
            out = []
            for h in range(nh):
                vt = jnp.concatenate([vt_ref[0, n_full * sub + u, head_cols(h), :] for u in range(r + 1)], axis=1)
                acc_ref[h] = jnp.dot(vt, ps[h].astype(BF16), preferred_element_type=F32)
                out += [ms[h], jnp.sum(ps[h], axis=0, keepdims=True)]
            scores0(3)
            return tuple(out) + tuple(cm)
        return f

    state = lax.switch(qi % sub, [diagonal(r) for r in range(sub)])

    def step(j, state):
        ml, cmax = state[:2 * nh], state[2 * nh:]
        out, cmax_next = [], []
        for h in range(nh):
            m, l = ml[2 * h], ml[2 * h + 1]
            d = slope_ref[h] * jnp.full((1, 2 * tq), j * tk - qi * tq, jnp.int32).astype(F32)
            m_new = jnp.maximum(m, cmax[h] + d)
            a = jnp.exp2(m - m_new)
            shift = m_new - d
            cm, lsum, pv = None, None, None
            for u in range(sub):
                rows = slice(u * tq, (u + 1) * tq)
                lhs = jnp.concatenate([k_ref[0, j + 1, rows, head_cols(h)], pos_ref[h, rows]], axis=1)
                s = jnp.dot(lhs, w_ref[h], preferred_element_type=F32)
                p = jnp.exp2(s_ref[h, rows] - shift)
                s_ref[h, rows] = s
                bm = jnp.max(s, axis=0, keepdims=True)
                cm = bm if cm is None else jnp.maximum(cm, bm)
                ps = jnp.sum(p, axis=0, keepdims=True)
                lsum = ps if lsum is None else lsum + ps
                part = jnp.dot(vt_ref[0, j * sub + u, head_cols(h), :], p.astype(BF16),
                               preferred_element_type=F32)
                pv = part if pv is None else pv + part
            acc_ref[h] = a * acc_ref[h] + pv
            out += [m_new, a * l + lsum]
            cmax_next.append(cm)
        return tuple(out) + tuple(cmax_next)

    state = lax.fori_loop(0, n_full - 1, step, state)
    ml, cmax = state[:2 * nh], state[2 * nh:]
    carry = lax.cond(n_full > 0, lambda: absorb(n_full - 1, s_ref, cmax, ml), lambda: ml)

    lam = _diff_lambda(lam_ref, lam_init)
    for h in range(nh):
        on = acc_ref[h] / carry[2 * h + 1]
        ot = on[:, 0:tq] - lam * on[:, tq:]
        ot = ot * lax.rsqrt(jnp.mean(ot * ot, axis=0, keepdims=True) + LN_EPS)
        o_ref[:, h * V_DIM:(h + 1) * V_DIM] = ot.T * g_ref[...] * (1.0 - lam_init)


def _attn_prompt(qt, kb, vt, lam_p, subln_g, *, lam_init):
    batch, nblk, _, tq = qt.shape
    tk, nh = ATTN_TK, HEADS_PER_STEP
    assert tq == ATTN_TQ and (nblk * tq) % tk == 0
    nchunk = nblk * tq // tk
    kb = kb.reshape(batch, nchunk, tk, D_MODEL)
    slopes = np.asarray(_slopes(), np.float32)
    c = np.arange(tk)
    pos = np.zeros((N_HEADS, tk, V_DIM), np.float32)
    n_terms = len(LOG2E_BF16_TERMS)
    for i in range(n_terms):
        pos[:, :, i] = slopes[:, None] * ((c // 256) * 256)[None]
        pos[:, :, n_terms + i] = slopes[:, None] * (c % 256)[None]
    pos = jnp.asarray(pos, BF16)
    rel = (c[:tq, None] - np.tile(np.arange(tq), 2)[None, :]).astype(np.float32)
    slope_row = np.ascontiguousarray(np.broadcast_to((slopes * LOG2E)[:, None, None], (N_HEADS, 1, 2 * tq)))
    return pl.pallas_call(
        functools.partial(_attn_prompt_kernel, lam_init=lam_init),
        grid=(batch, N_HEADS // nh, nblk),
        in_specs=[
            pl.BlockSpec((1, 1, nh * V_DIM, tq), lambda b, h, i: (b, i, h, 0)),
            pl.BlockSpec((1, nchunk, tk, nh * V_DIM), lambda b, h, i: (b, 0, 0, h)),
            pl.BlockSpec((1, nblk, nh * V_DIM, tq), lambda b, h, i: (b, 0, h, 0)),
            pl.BlockSpec((nh, tk, V_DIM), lambda b, h, i: (h, 0, 0)),
            pl.BlockSpec(rel.shape, lambda b, h, i: (0, 0), pipeline_mode=pl.Buffered(1)),
            pl.BlockSpec((nh, 1, 2 * tq), lambda b, h, i: (h, 0, 0)),
            pl.BlockSpec(lam_p.shape, lambda b, h, i: (0, 0)),
            pl.BlockSpec(subln_g.shape, lambda b, h, i: (0, 0)),
        ],
        out_specs=pl.BlockSpec((tq, nh * V_DIM), lambda b, h, i: (b * nblk + i, h)),
        out_shape=jax.ShapeDtypeStruct((batch * nblk * tq, D_MODEL), F32),
        scratch_shapes=[pltpu.VMEM((nh, 2 * V_DIM, 2 * tq), BF16), pltpu.VMEM((nh, V_DIM, 2 * tq), F32),
                        pltpu.VMEM((nh, tk, 2 * tq), F32)],
        compiler_params=_params(3),
        name="attn_prompt",
    )(qt, kb, vt, pos, rel, slope_row, lam_p, subln_g)


def _attn_decode_kernel(pt_ref, q_ref, kn_ref, vn_ref, ck_hbm, cv_hbm, bias_ref, biasn_ref, slope_ref, lam_ref,
                        g_ref, o_ref, kbuf, vbuf, sem, qm_ref, m_ref, l_ref, acc_ref, *, lam_init, past_len, n_tok):
    g_pages = PAGES_PER_STEP
    step = pl.program_id(1)
    n_steps = pl.num_programs(1)
    rows = n_tok * N_HEADS
    t = pl.program_id(0) * n_steps + step
    t_end = pl.num_programs(0) * n_steps

    def page_copies(tt, slot):
        b, s = tt // n_steps, tt % n_steps
        copies = []
        for g in range(g_pages):
            page = pt_ref[b, s * g_pages + g]
            copies.append(pltpu.make_async_copy(ck_hbm.at[page], kbuf.at[slot, g], sem.at[slot]))
            copies.append(pltpu.make_async_copy(cv_hbm.at[page], vbuf.at[slot, g], sem.at[slot]))
        return copies

    @pl.when(t == 0)
    def _():
        for ahead in range(PAGE_SLOTS - 1):
            for c in page_copies(ahead, ahead):
                c.start()

    nxt = t + (PAGE_SLOTS - 1)

    @pl.when(nxt < t_end)
    def _():
        for c in page_copies(nxt, nxt % PAGE_SLOTS):
            c.start()

    slot = t % PAGE_SLOTS
    for c in page_copies(t, slot):
        c.wait()

    @pl.when(step == 0)
    def _():
        q = q_ref[0]
        lane = lax.broadcasted_iota(jnp.int32, q.shape, 1)
        qm_ref[0:rows, :] = jnp.where(lane < HEAD_DIM, q, 0.0).astype(BF16)
        qm_ref[rows:, :] = jnp.where(lane >= HEAD_DIM, q, 0.0).astype(BF16)
        m_ref[...] = jnp.full_like(m_ref, NEG_INF)
        l_ref[...] = jnp.zeros_like(l_ref)
        acc_ref[...] = jnp.zeros_like(acc_ref)

    def update(pages):
        qm = qm_ref[...]
        s_list, mx = [], None
        for kp, _, bias, d in pages:
            s = lax.dot_general(qm, kp, (((1,), (1,)), ((), ())), preferred_element_type=F32) + bias
            s_list.append(s)
            blk = jnp.max(s, axis=1, keepdims=True) + d
            mx = blk if mx is None else jnp.maximum(mx, blk)
        m = m_ref[...]
        m_new = jnp.maximum(m, mx)
        a = jnp.exp2(m - m_new)
        l = a * l_ref[...]
        acc = a * acc_ref[...]
        for s, (_, vp, _, d) in zip(s_list, pages):
            p = jnp.exp2(s - (m_new - d))
            l = l + jnp.sum(p, axis=1, keepdims=True)
            acc = acc + jnp.dot(p.astype(BF16), vp, preferred_element_type=F32)
        l_ref[...] = l
        acc_ref[...] = acc
        m_ref[...] = m_new

    slope = slope_ref[...]
    pages = []
    for g in range(g_pages):
        page = step * g_pages + g
        d = slope * jnp.full((2 * rows, 1), page * PAGE_SIZE - past_len, jnp.int32).astype(F32)
        pages.append((kbuf[slot, g].astype(BF16), vbuf[slot, g].astype(BF16), bias_ref[...], d))
    update(pages)

    @pl.when(step == pl.num_programs(1) - 1)
    def _():
        update([(kn_ref[0].astype(BF16), vn_ref[0].astype(BF16), biasn_ref[...], jnp.zeros((2 * rows, 1), F32))])
        lam = _diff_lambda(lam_ref, lam_init)
        on = acc_ref[...] / l_ref[...]
        o = on[0:rows] - lam * on[rows:]
        o = o * lax.rsqrt(jnp.mean(o * o, axis=-1, keepdims=True) + LN_EPS)
        o_ref[0] = o * g_ref[...] * (1.0 - lam_init)


def _attn_decode(q, k_new, v_new, cache_k, cache_v, page_table, lam_p, subln_g, *, lam_init):
    db, rows, _ = q.shape
    n_tok = rows // N_HEADS
    n_pages = page_table.shape[1]
    past_len = n_pages * PAGE_SIZE
    g_pages = PAGES_PER_STEP
    assert n_pages % g_pages == 0 and db * (n_pages // g_pages) >= PAGE_SLOTS - 1
    slopes = np.asarray(_slopes(), np.float32)
    row_h = np.tile(np.arange(N_HEADS), 2 * n_tok)
    row_t = np.tile(np.repeat(np.arange(n_tok), N_HEADS), 2)
    row_slope = (slopes * LOG2E)[row_h]
    col_pos = np.repeat(np.arange(PAGE_SIZE), N_HEADS)
    col_h = np.tile(np.arange(N_HEADS), PAGE_SIZE)
    bias = np.where(row_h[:, None] == col_h[None, :],
                    row_slope[:, None] * (col_pos[None, :] - row_t[:, None]), NEG_INF).astype(np.float32)
    ncol_t = np.repeat(np.arange(n_tok), N_HEADS)
    ncol_h = np.tile(np.arange(N_HEADS), n_tok)
    dist = row_t[:, None] - ncol_t[None, :]
    bias_new = np.where((row_h[:, None] == ncol_h[None, :]) & (dist >= 0),
                        -row_slope[:, None] * dist, NEG_INF).astype(np.float32)
    slope_col = np.ascontiguousarray(row_slope[:, None])

    tok_spec = pl.BlockSpec((1, rows, V_DIM), lambda b, s, pt: (b, 0, 0))

    def const2(shape):
        return pl.BlockSpec(shape, lambda b, s, pt: (0, 0))

    hbm = pl.BlockSpec(memory_space=pl.ANY)
    page_block = (PAGE_SLOTS, g_pages, PAGE_SIZE * N_HEADS, V_DIM)
    grid_spec = pltpu.PrefetchScalarGridSpec(
        num_scalar_prefetch=1,
        grid=(db, n_pages // g_pages),
        in_specs=[tok_spec, tok_spec, tok_spec, hbm, hbm,
                  const2(bias.shape), const2(bias_new.shape), const2(slope_col.shape),
                  const2(lam_p.shape), const2(subln_g.shape)],
        out_specs=tok_spec,
        scratch_shapes=[pltpu.VMEM(page_block, F32), pltpu.VMEM(page_block, F32),
                        pltpu.SemaphoreType.DMA((PAGE_SLOTS,)),
                        pltpu.VMEM((2 * rows, V_DIM), BF16), pltpu.VMEM((2 * rows, 1), F32),
                        pltpu.VMEM((2 * rows, 1), F32), pltpu.VMEM((2 * rows, V_DIM), F32)],
    )
    return pl.pallas_call(
        functools.partial(_attn_decode_kernel, lam_init=lam_init, past_len=past_len, n_tok=n_tok),
        grid_spec=grid_spec,
        out_shape=jax.ShapeDtypeStruct((db, rows, V_DIM), F32),
        compiler_params=_params(2),
        name="attn_decode",
    )(page_table, q, k_new, v_new, cache_k, cache_v, bias, bias_new, slope_col, lam_p, subln_g)


def _mix_ffn_kernel(h_ref, o_ref, sga_ref, sgm_ref, gu_ref, vn_ref, wsp_ref, bsp_ref, wo_ref,
                    g2_ref, b2_ref, wg_ref, wu_ref, wd_ref, g3_ref, b3_ref, y_ref, s_ref, *, period):
    tm = h_ref.shape[0]
    r = lax.broadcasted_iota(jnp.int32, (CHUNK, CHUNK), 0)
    c = lax.broadcasted_iota(jnp.int32, (CHUNK, CHUNK), 1)
    keep = (r // period == c // period) & (c <= r)
    for grp in range(N_GROUPS):
        w = jnp.where(keep, wsp_ref[grp], 0.0).astype(BF16)
        cols = slice(grp * GROUP_WIDTH, (grp + 1) * GROUP_WIDTH)
        for ch in range(tm // CHUNK):
            rws = slice(ch * CHUNK, (ch + 1) * CHUNK)
            f = jnp.dot(w, vn_ref[rws, cols].astype(BF16), preferred_element_type=F32) + bsp_ref[grp]
            s_ref[rws, cols] = gu_ref[rws, cols] * f
    mix_in = sga_ref[...] * o_ref[...] + sgm_ref[...] * s_ref[...]
    mix = jnp.dot(mix_in.astype(BF16), wo_ref[...], preferred_element_type=F32)
    h2 = _layer_norm(ALPHA * h_ref[...] + mix, g2_ref[...], b2_ref[...])
    y = _swiglu(h2.astype(BF16), wg_ref, wu_ref, wd_ref)
    y_ref[...] = _layer_norm(ALPHA * h2 + 0.5 * y, g3_ref[...], b3_ref[...])


def _mix_ffn(h, o, sga, sgm, gu, vn, wsp, bsp, wo, g2, b2, wg, wu, wd, g3, b3, *, tm, period):
    rows = h.shape[0]
    row_spec = pl.BlockSpec((tm, D_MODEL), lambda i: (i, 0))
    consts = (wsp, bsp, wo, g2, b2, wg, wu, wd, g3, b3)
    return pl.pallas_call(
        functools.partial(_mix_ffn_kernel, period=period),
        grid=(rows // tm,),
        in_specs=[row_spec] * 6 + [_const_spec(a.shape) for a in consts],
        out_specs=row_spec,
        out_shape=jax.ShapeDtypeStruct((rows, D_MODEL), F32),
        scratch_shapes=[pltpu.VMEM((tm, D_MODEL), F32)],
        compiler_params=_params(1),
        name="mix_ffn",
    )(h, o, sga, sgm, gu, vn, *consts)


def _spatial_params(w_spatial, b_spatial, period):
    if period == CHUNK:
        w, b = w_spatial, b_spatial
    else:
        sel = np.equal.outer(np.arange(CHUNK) % period, np.arange(period)).astype(np.float32)
        hi = lax.Precision.HIGHEST
        w = jnp.einsum("ra,gab,cb->grc", sel, w_spatial[:, :period, :period], sel, precision=hi)
        b = jnp.einsum("ra,ga->gr", sel, b_spatial[:, :period], precision=hi)
    return w, b[:, :, None]


def kernel(x_prompt, x_sample, cache_k, cache_v, page_table, ln1_g, ln1_b, ffn1_wg, ffn1_wu, ffn1_wd, w_in, lam_q1, lam_k1, lam_q2, lam_k2, subln_g, gmlp_ln_g, gmlp_ln_b, w_spatial, b_spatial, w_out, ln2_g, ln2_b, ffn2_wg, ffn2_wu, ffn2_wd, ln3_g, ln3_b):
    batch, seq, _ = x_prompt.shape
    db, n_tok, _ = x_sample.shape
    rows_p, rows_s = batch * seq, db * n_tok
    assert rows_s == CHUNK and CHUNK % n_tok == 0 and seq % ATTN_TK == 0
    hp = x_prompt.reshape(rows_p, D_MODEL)
    hs = x_sample.reshape(rows_s, D_MODEL)
    row = lambda a: a.reshape(1, -1)
    tm = ATTN_TQ
    kp_l, vp_l, ks_l, vs_l, gv_l = [], [], [], [], []
    for l in range(DEPTH):
        lam_init = 0.8 - 0.6 * math.exp(-0.3 * l)
        wg1, wu1, wd1 = ffn1_wg[l].astype(BF16), ffn1_wu[l].astype(BF16), ffn1_wd[l].astype(BF16)
        wg2, wu2, wd2 = ffn2_wg[l].astype(BF16), ffn2_wu[l].astype(BF16), ffn2_wd[l].astype(BF16)
        w_in_b, w_out_b = w_in[l].astype(BF16), w_out[l].astype(BF16)
        lam_p = jnp.stack([lam_q1[l], lam_k1[l], lam_q2[l], lam_k2[l]])
        sub_g = row(subln_g[l])
        ffn1 = (wg1, wu1, wd1, row(ln1_g[l]), row(ln1_b[l]))
        tail = (w_out_b, row(ln2_g[l]), row(ln2_b[l]), wg2, wu2, wd2, row(ln3_g[l]), row(ln3_b[l]))
        gln = (row(gmlp_ln_g[l]), row(gmlp_ln_b[l]))

        h1 = _ffn_ln(hp, *ffn1, tm=tm)
        k_p, v_p, qt, kb, vt, gu, vn, sga, sgm = _proj(h1, w_in_b, *gln, tm=tm, batch=batch)
        o = _attn_prompt(qt, kb, vt, lam_p, sub_g, lam_init=lam_init)
        wsp, bsp = _spatial_params(w_spatial[l], b_spatial[l], CHUNK)
        hp = _mix_ffn(h1, o, sga, sgm, gu, vn, wsp, bsp, *tail, tm=tm, period=CHUNK)

        h1 = _ffn_ln(hs, *ffn1, tm=rows_s)
        q_s, k_s, v_s, gu, vn_s, sga, sgm = _proj(h1, w_in_b, *gln, tm=rows_s)
        tok = lambda a: a.reshape(db, n_tok * N_HEADS, V_DIM)
        o = _attn_decode(tok(q_s), tok(k_s), tok(v_s),
                         cache_k[l].reshape(-1, PAGE_SIZE * N_HEADS, V_DIM),
                         cache_v[l].reshape(-1, PAGE_SIZE * N_HEADS, V_DIM),
                         page_table, lam_p, sub_g, lam_init=lam_init)
        wsp, bsp = _spatial_params(w_spatial[l], b_spatial[l], n_tok)
        hs = _mix_ffn(h1, o.reshape(rows_s, D_MODEL), sga, sgm, gu, vn_s, wsp, bsp, *tail, tm=rows_s, period=n_tok)

        kp_l.append(k_p.reshape(batch, seq, N_HEADS, V_DIM))
        vp_l.append(v_p.reshape(batch, seq, N_HEADS, V_DIM))
        ks_l.append(k_s.reshape(db, n_tok, N_HEADS, V_DIM))
        vs_l.append(v_s.reshape(db, n_tok, N_HEADS, V_DIM))
        gv_l.append(vn_s.reshape(db, n_tok, D_MODEL))
    return (hp.reshape(batch, seq, D_MODEL), hs.reshape(db, n_tok, D_MODEL),
            jnp.stack(kp_l), jnp.stack(vp_l), jnp.stack(ks_l), jnp.stack(vs_l), jnp.stack(gv_l))
```

```python
import functools
import math

import jax
import jax.numpy as jnp
import numpy as np
from jax import lax
from jax.experimental import pallas as pl
from jax.experimental.pallas import tpu as pltpu

D_MODEL = 1024
DEPTH = 1
PAGE_SIZE = 128
N_HEADS = 8
HEAD_DIM = 64
V_DIM = 2 * HEAD_DIM
CHUNK = 128
GROUP_WIDTH = 128
N_GROUPS = D_MODEL // GROUP_WIDTH
D_FF = 2816
ALPHA = (2 * DEPTH) ** 0.25
LN_EPS = 1e-5
NEG_INF = -1e30
QK_SCALE = HEAD_DIM ** -0.5
LOG2E = math.log2(math.e)
Q_PRESCALE = QK_SCALE * LOG2E

VMEM_LIMIT_BYTES = 56 * 1024 * 1024
ATTN_TQ = 256
ATTN_TK = 1024
HEADS_PER_STEP = 2
PAGES_PER_STEP = 8
PAGE_SLOTS = 3

F32 = jnp.float32
BF16 = jnp.bfloat16


def _bf16_terms(x, n):
    terms = []
    for _ in range(n):
        m, e = math.frexp(x)
        t = math.ldexp(round(m * 256) / 256, e)
        terms.append(t)
        x -= t
    return tuple(terms)


LOG2E_BF16_TERMS = _bf16_terms(LOG2E, 4)


def _slopes():
    return [2.0 ** (-8.0 * (h + 1) / N_HEADS) for h in range(N_HEADS)]


def _layer_norm(x, g, b):
    mu = jnp.mean(x, axis=-1, keepdims=True)
    xc = x - mu
    var = jnp.mean(xc * xc, axis=-1, keepdims=True)
    return xc * lax.rsqrt(var + LN_EPS) * g + b


def _gelu(x):
    return 0.5 * x * (1.0 + lax.erf(x * math.sqrt(0.5)))


def _swiglu(xb, wg_ref, wu_ref, wd_ref):
    gate = jnp.dot(xb, wg_ref[...], preferred_element_type=F32)
    up = jnp.dot(xb, wu_ref[...], preferred_element_type=F32)
    act = (gate * jax.nn.sigmoid(gate)) * up
    return jnp.dot(act.astype(BF16), wd_ref[...], preferred_element_type=F32)


def _diff_lambda(lam_ref, lam_init):
    lp = lam_ref[...]
    a = jnp.sum(lp[0:1] * lp[1:2], axis=-1, keepdims=True)
    b = jnp.sum(lp[2:3] * lp[3:4], axis=-1, keepdims=True)
    return jnp.exp(a) - jnp.exp(b) + lam_init


def _const_spec(shape):
    nd = len(shape)
    return pl.BlockSpec(shape, lambda *_: (0,) * nd, pipeline_mode=pl.Buffered(1))


def _params(n_axes):
    return pltpu.CompilerParams(
        dimension_semantics=("arbitrary",) * n_axes, vmem_limit_bytes=VMEM_LIMIT_BYTES)


def _ffn_ln_kernel(x_ref, wg_ref, wu_ref, wd_ref, g_ref, b_ref, o_ref):
    x = x_ref[...]
    y = _swiglu(x.astype(BF16), wg_ref, wu_ref, wd_ref)
    o_ref[...] = _layer_norm(ALPHA * x + 0.5 * y, g_ref[...], b_ref[...])


def _ffn_ln(x, wg, wu, wd, g, b, *, tm):
    rows = x.shape[0]
    row_spec = pl.BlockSpec((tm, D_MODEL), lambda i: (i, 0))
    return pl.pallas_call(
        _ffn_ln_kernel,
        grid=(rows // tm,),
        in_specs=[row_spec, _const_spec(wg.shape), _const_spec(wu.shape), _const_spec(wd.shape),
                  _const_spec(g.shape), _const_spec(b.shape)],
        out_specs=row_spec,
        out_shape=jax.ShapeDtypeStruct((rows, D_MODEL), F32),
        compiler_params=_params(1),
        name="ffn_ln",
    )(x, wg, wu, wd, g, b)


def _proj_kernel(h_ref, w_ref, lg_ref, lb_ref, *out_refs, attn_layouts):
    hb = h_ref[...].astype(BF16)

    def section(i):
        return jnp.dot(hb, w_ref[:, i * D_MODEL:(i + 1) * D_MODEL], preferred_element_type=F32)

    q = section(0) * Q_PRESCALE
    k = section(1)
    v = section(2)
    if attn_layouts:
        k_ref, v_ref, qt_ref, kb_ref, vt_ref, gu_ref, vn_ref, sga_ref, sgm_ref = out_refs
        qt_ref[0, 0] = q.T.astype(BF16)
        kb_ref[...] = k.astype(BF16)
        vt_ref[0, 0] = v.T.astype(BF16)
    else:
        q_ref, k_ref, v_ref, gu_ref, vn_ref, sga_ref, sgm_ref = out_refs
        q_ref[...] = q
    k_ref[...] = k
    v_ref[...] = v
    gu_ref[...] = _gelu(section(3))
    vn_ref[...] = _layer_norm(_gelu(section(4)), lg_ref[...], lb_ref[...])
    sga_ref[...] = jax.nn.sigmoid(section(5))
    sgm_ref[...] = jax.nn.sigmoid(section(6))


def _proj(h, w_in, ln_g, ln_b, *, tm, batch=None):
    rows = h.shape[0]
    n_tiles = rows // tm
    row_spec = pl.BlockSpec((tm, D_MODEL), lambda i: (i, 0))
    row_f32 = jax.ShapeDtypeStruct((rows, D_MODEL), F32)
    attn_layouts = batch is not None
    if attn_layouts:
        per_b = n_tiles // batch
        t_spec = pl.BlockSpec((1, 1, D_MODEL, tm), lambda i: (i // per_b, i % per_b, 0, 0))
        t_shape = jax.ShapeDtypeStruct((batch, per_b, D_MODEL, tm), BF16)
        out_specs = [row_spec, row_spec, t_spec, row_spec, t_spec] + [row_spec] * 4
        out_shape = [row_f32, row_f32, t_shape, jax.ShapeDtypeStruct((rows, D_MODEL), BF16), t_shape] + [row_f32] * 4
    else:
        out_specs = [row_spec] * 7
        out_shape = [row_f32] * 7
    return pl.pallas_call(
        functools.partial(_proj_kernel, attn_layouts=attn_layouts),
        grid=(n_tiles,),
        in_specs=[row_spec, _const_spec(w_in.shape), _const_spec(ln_g.shape), _const_spec(ln_b.shape)],
        out_specs=out_specs,
        out_shape=out_shape,
        compiler_params=_params(1),
        name="proj",
    )(h, w_in, ln_g, ln_b)


def _attn_prompt_kernel(qt_ref, k_ref, vt_ref, pos_ref, rel_ref, slope_ref, lam_ref, g_ref, o_ref,
                        w_ref, acc_ref, s_ref, *, lam_init):
    tq, tk, nh = ATTN_TQ, ATTN_TK, HEADS_PER_STEP
    sub = tk // tq
    qi = pl.program_id(2)
    n_full = qi // sub
    zeros = jnp.zeros((HEAD_DIM, tq), BF16)
    prow = lax.broadcasted_iota(jnp.int32, (V_DIM, 2 * tq), 0)
    pos_rows = jnp.zeros((V_DIM, 2 * tq), F32)
    for i, term in enumerate(LOG2E_BF16_TERMS * 2):
        pos_rows = jnp.where(prow == i, term, pos_rows)
    pos_rows = pos_rows.astype(BF16)
    for h in range(nh):
        qt = qt_ref[0, 0, h * V_DIM:(h + 1) * V_DIM, :]
        w_ref[h, 0:HEAD_DIM, 0:tq] = qt[0:HEAD_DIM]
        w_ref[h, HEAD_DIM:V_DIM, 0:tq] = zeros
        w_ref[h, 0:HEAD_DIM, tq:] = zeros
        w_ref[h, HEAD_DIM:V_DIM, tq:] = qt[HEAD_DIM:]
        w_ref[h, V_DIM:, :] = pos_rows

    def head_cols(h):
        return slice(h * V_DIM, (h + 1) * V_DIM)

    def with_ones(vt):
        return jnp.concatenate([vt, jnp.ones((16, vt.shape[1]), BF16)], axis=0)

    def absorb(j, src_ref, cmax, ml):
        out = []
        for h in range(nh):
            m, l = ml[2 * h], ml[2 * h + 1]
            d = slope_ref[h] * jnp.full((1, 2 * tq), j * tk - qi * tq, jnp.int32).astype(F32)
            m_new = jnp.maximum(m, cmax[h] + d)
            a = jnp.exp2(m - m_new)
            p = jnp.exp2(src_ref[h] - (m_new - d))
            vt = jnp.concatenate([vt_ref[0, j * sub + u, head_cols(h), :] for u in range(sub)], axis=1)
            pv = jnp.dot(with_ones(vt), p.astype(BF16), preferred_element_type=F32)
            acc_ref[h] = a * acc_ref[h] + pv[:V_DIM]
            out += [m_new, a * l + pv[V_DIM:V_DIM + 1]]
        return tuple(out)

    def diagonal(r):
        def f():
            nk = (r + 1) * tq
            cm = [None] * nh

            def scores0(u):
                rows = slice(u * tq, (u + 1) * tq)
                for h in range(nh):
                    lhs = jnp.concatenate([k_ref[0, 0, rows, head_cols(h)], pos_ref[h, rows]], axis=1)
                    s0 = jnp.dot(lhs, w_ref[h], preferred_element_type=F32)
                    s_ref[h, rows] = s0
                    bm = jnp.max(s0, axis=0, keepdims=True)
                    cm[h] = bm if cm[h] is None else jnp.maximum(cm[h], bm)

            s_d = []
            for h in range(nh):
                lhs = jnp.concatenate([k_ref[0, n_full, 0:nk, head_cols(h)], pos_ref[h, 0:nk]], axis=1)
                s_d.append(jnp.dot(lhs, w_ref[h], preferred_element_type=F32))
            scores0(0)
            ms, ds = [], []
            for h in range(nh):
                tail = jnp.where(rel_ref[...] <= 0.0, s_d[h][r * tq:], NEG_INF)
                s_d[h] = tail if r == 0 else jnp.concatenate([s_d[h][:r * tq], tail], axis=0)
                d = slope_ref[h] * float(-r * tq)
                ms.append(jnp.max(s_d[h], axis=0, keepdims=True) + d)
                ds.append(d)
            scores0(1)
            ps = [jnp.exp2(s_d[h] - (ms[h] - ds[h])) for h in range(nh)]
            scores0(2)
            out = []
            for h in range(nh):
                vt = jnp.concatenate([vt_ref[0, n_full * sub + u, head_cols(h), :] for u in range(r + 1)], axis=1)
                pv = jnp.dot(with_ones(vt), ps[h].astype(BF16), preferred_element_type=F32)
                acc_ref[h] = pv[:V_DIM]
                out += [ms[h], pv[V_DIM:V_DIM + 1]]
            scores0(3)
            return tuple(out) + tuple(cm)
        return f

    state = lax.switch(qi % sub, [diagonal(r) for r in range(sub)])

    def step(j, state):
        ml, cmax = state[:2 * nh], state[2 * nh:]
        out, cmax_next = [], []
        for h in range(nh):
            m, l = ml[2 * h], ml[2 * h + 1]
            d = slope_ref[h] * jnp.full((1, 2 * tq), j * tk - qi * tq, jnp.int32).astype(F32)
            m_new = jnp.maximum(m, cmax[h] + d)
            a = jnp.exp2(m - m_new)
            shift = m_new - d
            cm, lsum, pv = None, None, None
            for u in range(sub):
                rows = slice(u * tq, (u + 1) * tq)
                lhs = jnp.concatenate([k_ref[0, j + 1, rows, head_cols(h)], pos_ref[h, rows]], axis=1)
                s = jnp.dot(lhs, w_ref[h], preferred_element_type=F32)
                p = jnp.exp2(s_ref[h, rows] - shift)
                s_ref[h, rows] = s
                bm = jnp.max(s, axis=0, keepdims=True)
                cm = bm if cm is None else jnp.maximum(cm, bm)
                part = jnp.dot(with_ones(vt_ref[0, j * sub + u, head_cols(h), :]), p.astype(BF16),
                               preferred_element_type=F32)
                pv = part if pv is None else pv + part
            lsum = pv[V_DIM:V_DIM + 1]
            pv = pv[:V_DIM]
            acc_ref[h] = a * acc_ref[h] + pv
            out += [m_new, a * l + lsum]
            cmax_next.append(cm)
        return tuple(out) + tuple(cmax_next)

    state = lax.fori_loop(0, n_full - 1, step, state)
    ml, cmax = state[:2 * nh], state[2 * nh:]
    carry = lax.cond(n_full > 0, lambda: absorb(n_full - 1, s_ref, cmax, ml), lambda: ml)

    lam = _diff_lambda(lam_ref, lam_init)
    for h in range(nh):
        on = acc_ref[h] / carry[2 * h + 1]
        ot = on[:, 0:tq] - lam * on[:, tq:]
        ot = ot * lax.rsqrt(jnp.mean(ot * ot, axis=0, keepdims=True) + LN_EPS)
        o_ref[:, h * V_DIM:(h + 1) * V_DIM] = ot.T * g_ref[...] * (1.0 - lam_init)


def _attn_prompt(qt, kb, vt, lam_p, subln_g, *, lam_init):
    batch, nblk, _, tq = qt.shape
    tk, nh = ATTN_TK, HEADS_PER_STEP
    assert tq == ATTN_TQ and (nblk * tq) % tk == 0
    nchunk = nblk * tq // tk
    kb = kb.reshape(batch, nchunk, tk, D_MODEL)
    slopes = np.asarray(_slopes(), np.float32)
    c = np.arange(tk)
    pos = np.zeros((N_HEADS, tk, V_DIM), np.float32)
    n_terms = len(LOG2E_BF16_TERMS)
    for i in range(n_terms):
        pos[:, :, i] = slopes[:, None] * ((c // 256) * 256)[None]
        pos[:, :, n_terms + i] = slopes[:, None] * (c % 256)[None]
    pos = jnp.asarray(pos, BF16)
    rel = (c[:tq, None] - np.tile(np.arange(tq), 2)[None, :]).astype(np.float32)
    slope_row = np.ascontiguousarray(np.broadcast_to((slopes * LOG2E)[:, None, None], (N_HEADS, 1, 2 * tq)))
    return pl.pallas_call(
        functools.partial(_attn_prompt_kernel, lam_init=lam_init),
        grid=(batch, N_HEADS // nh, nblk),
        in_specs=[
            pl.BlockSpec((1, 1, nh * V_DIM, tq), lambda b, h, i: (b, i, h, 0)),
            pl.BlockSpec((1, nchunk, tk, nh * V_DIM), lambda b, h, i: (b, 0, 0, h)),
            pl.BlockSpec((1, nblk, nh * V_DIM, tq), lambda b, h, i: (b, 0, h, 0)),
            pl.BlockSpec((nh, tk, V_DIM), lambda b, h, i: (h, 0, 0)),
            pl.BlockSpec(rel.shape, lambda b, h, i: (0, 0), pipeline_mode=pl.Buffered(1)),
            pl.BlockSpec((nh, 1, 2 * tq), lambda b, h, i: (h, 0, 0)),
            pl.BlockSpec(lam_p.shape, lambda b, h, i: (0, 0)),
            pl.BlockSpec(subln_g.shape, lambda b, h, i: (0, 0)),
        ],
        out_specs=pl.BlockSpec((tq, nh * V_DIM), lambda b, h, i: (b * nblk + i, h)),
        out_shape=jax.ShapeDtypeStruct((batch * nblk * tq, D_MODEL), F32),
        scratch_shapes=[pltpu.VMEM((nh, 2 * V_DIM, 2 * tq), BF16), pltpu.VMEM((nh, V_DIM, 2 * tq), F32),
                        pltpu.VMEM((nh, tk, 2 * tq), F32)],
        compiler_params=_params(3),
        name="attn_prompt",
    )(qt, kb, vt, pos, rel, slope_row, lam_p, subln_g)


def _attn_decode_kernel(pt_ref, q_ref, kn_ref, vn_ref, ck_hbm, cv_hbm, bias_ref, biasn_ref, slope_ref, lam_ref,
                        g_ref, o_ref, kbuf, vbuf, sem, qm_ref, m_ref, l_ref, acc_ref, *, lam_init, past_len, n_tok):
    g_pages = PAGES_PER_STEP
    step = pl.program_id(1)
    n_steps = pl.num_programs(1)
    rows = n_tok * N_HEADS
    t = pl.program_id(0) * n_steps + step
    t_end = pl.num_programs(0) * n_steps

    def page_copies(tt, slot):
        b, s = tt // n_steps, tt % n_steps
        copies = []
        for g in range(g_pages):
            page = pt_ref[b, s * g_pages + g]
            copies.append(pltpu.make_async_copy(ck_hbm.at[page], kbuf.at[slot, g], sem.at[slot]))
            copies.append(pltpu.make_async_copy(cv_hbm.at[page], vbuf.at[slot, g], sem.at[slot]))
        return copies

    @pl.when(t == 0)
    def _():
        for ahead in range(PAGE_SLOTS - 1):
            for c in page_copies(ahead, ahead):
                c.start()

    nxt = t + (PAGE_SLOTS - 1)

    @pl.when(nxt < t_end)
    def _():
        for c in page_copies(nxt, nxt % PAGE_SLOTS):
            c.start()

    slot = t % PAGE_SLOTS
    for c in page_copies(t, slot):
        c.wait()

    @pl.when(step == 0)
    def _():
        q = q_ref[0]
        lane = lax.broadcasted_iota(jnp.int32, q.shape, 1)
        qm_ref[0:rows, :] = jnp.where(lane < HEAD_DIM, q, 0.0).astype(BF16)
        qm_ref[rows:, :] = jnp.where(lane >= HEAD_DIM, q, 0.0).astype(BF16)
        m_ref[...] = jnp.full_like(m_ref, NEG_INF)
        l_ref[...] = jnp.zeros_like(l_ref)
        acc_ref[...] = jnp.zeros_like(acc_ref)

    def update(pages):
        qm = qm_ref[...]
        s_list, mx = [], None
        for kp, _, bias, d in pages:
            s = lax.dot_general(qm, kp, (((1,), (1,)), ((), ())), preferred_element_type=F32) + bias
            s_list.append(s)
            blk = jnp.max(s, axis=1, keepdims=True) + d
            mx = blk if mx is None else jnp.maximum(mx, blk)
        m = m_ref[...]
        m_new = jnp.maximum(m, mx)
        a = jnp.exp2(m - m_new)
        l = a * l_ref[...]
        acc = a * acc_ref[...]
        for s, (_, vp, _, d) in zip(s_list, pages):
            p = jnp.exp2(s - (m_new - d))
            l = l + jnp.sum(p, axis=1, keepdims=True)
            acc = acc + jnp.dot(p.astype(BF16), vp, preferred_element_type=F32)
        l_ref[...] = l
        acc_ref[...] = acc
        m_ref[...] = m_new

    slope = slope_ref[...]
    pages = []
    for g in range(g_pages):
        page = step * g_pages + g
        d = slope * jnp.full((2 * rows, 1), page * PAGE_SIZE - past_len, jnp.int32).astype(F32)
        pages.append((kbuf[slot, g].astype(BF16), vbuf[slot, g].astype(BF16), bias_ref[...], d))
    update(pages)

    @pl.when(step == pl.num_programs(1) - 1)
    def _():
        update([(kn_ref[0].astype(BF16), vn_ref[0].astype(BF16), biasn_ref[...], jnp.zeros((2 * rows, 1), F32))])
        lam = _diff_lambda(lam_ref, lam_init)
        on = acc_ref[...] / l_ref[...]
        o = on[0:rows] - lam * on[rows:]
        o = o * lax.rsqrt(jnp.mean(o * o, axis=-1, keepdims=True) + LN_EPS)
        o_ref[0] = o * g_ref[...] * (1.0 - lam_init)


def _attn_decode(q, k_new, v_new, cache_k, cache_v, page_table, lam_p, subln_g, *, lam_init):
    db, rows, _ = q.shape
    n_tok = rows // N_HEADS
    n_pages = page_table.shape[1]
    past_len = n_pages * PAGE_SIZE
    g_pages = PAGES_PER_STEP
    assert n_pages % g_pages == 0 and db * (n_pages // g_pages) >= PAGE_SLOTS - 1
    slopes = np.asarray(_slopes(), np.float32)
    row_h = np.tile(np.arange(N_HEADS), 2 * n_tok)
    row_t = np.tile(np.repeat(np.arange(n_tok), N_HEADS), 2)
    row_slope = (slopes * LOG2E)[row_h]
    col_pos = np.repeat(np.arange(PAGE_SIZE), N_HEADS)
    col_h = np.tile(np.arange(N_HEADS), PAGE_SIZE)
    bias = np.where(row_h[:, None] == col_h[None, :],
                    row_slope[:, None] * (col_pos[None, :] - row_t[:, None]), NEG_INF).astype(np.float32)
    ncol_t = np.repeat(np.arange(n_tok), N_HEADS)
    ncol_h = np.tile(np.arange(N_HEADS), n_tok)
    dist = row_t[:, None] - ncol_t[None, :]
    bias_new = np.where((row_h[:, None] == ncol_h[None, :]) & (dist >= 0),
                        -row_slope[:, None] * dist, NEG_INF).astype(np.float32)
    slope_col = np.ascontiguousarray(row_slope[:, None])

    tok_spec = pl.BlockSpec((1, rows, V_DIM), lambda b, s, pt: (b, 0, 0))

    def const2(shape):
        return pl.BlockSpec(shape, lambda b, s, pt: (0, 0))

    hbm = pl.BlockSpec(memory_space=pl.ANY)
    page_block = (PAGE_SLOTS, g_pages, PAGE_SIZE * N_HEADS, V_DIM)
    grid_spec = pltpu.PrefetchScalarGridSpec(
        num_scalar_prefetch=1,
        grid=(db, n_pages // g_pages),
        in_specs=[tok_spec, tok_spec, tok_spec, hbm, hbm,
                  const2(bias.shape), const2(bias_new.shape), const2(slope_col.shape),
                  const2(lam_p.shape), const2(subln_g.shape)],
        out_specs=tok_spec,
        scratch_shapes=[pltpu.VMEM(page_block, F32), pltpu.VMEM(page_block, F32),
                        pltpu.SemaphoreType.DMA((PAGE_SLOTS,)),
                        pltpu.VMEM((2 * rows, V_DIM), BF16), pltpu.VMEM((2 * rows, 1), F32),
                        pltpu.VMEM((2 * rows, 1), F32), pltpu.VMEM((2 * rows, V_DIM), F32)],
    )
    return pl.pallas_call(
        functools.partial(_attn_decode_kernel, lam_init=lam_init, past_len=past_len, n_tok=n_tok),
        grid_spec=grid_spec,
        out_shape=jax.ShapeDtypeStruct((db, rows, V_DIM), F32),
        compiler_params=_params(2),
        name="attn_decode",
    )(page_table, q, k_new, v_new, cache_k, cache_v, bias, bias_new, slope_col, lam_p, subln_g)


def _mix_ffn_kernel(h_ref, o_ref, sga_ref, sgm_ref, gu_ref, vn_ref, wsp_ref, bsp_ref, wo_ref,
                    g2_ref, b2_ref, wg_ref, wu_ref, wd_ref, g3_ref, b3_ref, y_ref, s_ref, *, period):
    tm = h_ref.shape[0]
    r = lax.broadcasted_iota(jnp.int32, (CHUNK, CHUNK), 0)
    c = lax.broadcasted_iota(jnp.int32, (CHUNK, CHUNK), 1)
    keep = (r // period == c // period) & (c <= r)
    for grp in range(N_GROUPS):
        w = jnp.where(keep, wsp_ref[grp], 0.0).astype(BF16)
        cols = slice(grp * GROUP_WIDTH, (grp + 1) * GROUP_WIDTH)
        for ch in range(tm // CHUNK):
            rws = slice(ch * CHUNK, (ch + 1) * CHUNK)
            f = jnp.dot(w, vn_ref[rws, cols].astype(BF16), preferred_element_type=F32) + bsp_ref[grp]
            s_ref[rws, cols] = gu_ref[rws, cols] * f
    mix_in = sga_ref[...] * o_ref[...] + sgm_ref[...] * s_ref[...]
    mix = jnp.dot(mix_in.astype(BF16), wo_ref[...], preferred_element_type=F32)
    h2 = _layer_norm(ALPHA * h_ref[...] + mix, g2_ref[...], b2_ref[...])
    y = _swiglu(h2.astype(BF16), wg_ref, wu_ref, wd_ref)
    y_ref[...] = _layer_norm(ALPHA * h2 + 0.5 * y, g3_ref[...], b3_ref[...])


def _mix_ffn(h, o, sga, sgm, gu, vn, wsp, bsp, wo, g2, b2, wg, wu, wd, g3, b3, *, tm, period):
    rows = h.shape[0]
    row_spec = pl.BlockSpec((tm, D_MODEL), lambda i: (i, 0))
    consts = (wsp, bsp, wo, g2, b2, wg, wu, wd, g3, b3)
    return pl.pallas_call(
        functools.partial(_mix_ffn_kernel, period=period),
        grid=(rows // tm,),
        in_specs=[row_spec] * 6 + [_const_spec(a.shape) for a in consts],
        out_specs=row_spec,
        out_shape=jax.ShapeDtypeStruct((rows, D_MODEL), F32),
        scratch_shapes=[pltpu.VMEM((tm, D_MODEL), F32)],
        compiler_params=_params(1),
        name="mix_ffn",
    )(h, o, sga, sgm, gu, vn, *consts)


def _spatial_params(w_spatial, b_spatial, period):
    if period == CHUNK:
        w, b = w_spatial, b_spatial
    else:
        sel = np.equal.outer(np.arange(CHUNK) % period, np.arange(period)).astype(np.float32)
        hi = lax.Precision.HIGHEST
        w = jnp.einsum("ra,gab,cb->grc", sel, w_spatial[:, :period, :period], sel, precision=hi)
        b = jnp.einsum("ra,ga->gr", sel, b_spatial[:, :period], precision=hi)
    return w, b[:, :, None]


def kernel(x_prompt, x_sample, cache_k, cache_v, page_table, ln1_g, ln1_b, ffn1_wg, ffn1_wu, ffn1_wd, w_in, lam_q1, lam_k1, lam_q2, lam_k2, subln_g, gmlp_ln_g, gmlp_ln_b, w_spatial, b_spatial, w_out, ln2_g, ln2_b, ffn2_wg, ffn2_wu, ffn2_wd, ln3_g, ln3_b):
    batch, seq, _ = x_prompt.shape
    db, n_tok, _ = x_sample.shape
    rows_p, rows_s = batch * seq, db * n_tok
    assert rows_s == CHUNK and CHUNK % n_tok == 0 and seq % ATTN_TK == 0
    hp = x_prompt.reshape(rows_p, D_MODEL)
    hs = x_sample.reshape(rows_s, D_MODEL)
    row = lambda a: a.reshape(1, -1)
    tm = ATTN_TQ
    kp_l, vp_l, ks_l, vs_l, gv_l = [], [], [], [], []
    for l in range(DEPTH):
        lam_init = 0.8 - 0.6 * math.exp(-0.3 * l)
        wg1, wu1, wd1 = ffn1_wg[l].astype(BF16), ffn1_wu[l].astype(BF16), ffn1_wd[l].astype(BF16)
        wg2, wu2, wd2 = ffn2_wg[l].astype(BF16), ffn2_wu[l].astype(BF16), ffn2_wd[l].astype(BF16)
        w_in_b, w_out_b = w_in[l].astype(BF16), w_out[l].astype(BF16)
        lam_p = jnp.stack([lam_q1[l], lam_k1[l], lam_q2[l], lam_k2[l]])
        sub_g = row(subln_g[l])
        ffn1 = (wg1, wu1, wd1, row(ln1_g[l]), row(ln1_b[l]))
        tail = (w_out_b, row(ln2_g[l]), row(ln2_b[l]), wg2, wu2, wd2, row(ln3_g[l]), row(ln3_b[l]))
        gln = (row(gmlp_ln_g[l]), row(gmlp_ln_b[l]))

        h1 = _ffn_ln(hp, *ffn1, tm=tm)
        k_p, v_p, qt, kb, vt, gu, vn, sga, sgm = _proj(h1, w_in_b, *gln, tm=tm, batch=batch)
        o = _attn_prompt(qt, kb, vt, lam_p, sub_g, lam_init=lam_init)
        wsp, bsp = _spatial_params(w_spatial[l], b_spatial[l], CHUNK)
        hp = _mix_ffn(h1, o, sga, sgm, gu, vn, wsp, bsp, *tail, tm=tm, period=CHUNK)

        h1 = _ffn_ln(hs, *ffn1, tm=rows_s)
        q_s, k_s, v_s, gu, vn_s, sga, sgm = _proj(h1, w_in_b, *gln, tm=rows_s)
        tok = lambda a: a.reshape(db, n_tok * N_HEADS, V_DIM)
        o = _attn_decode(tok(q_s), tok(k_s), tok(v_s),
                         cache_k[l].reshape(-1, PAGE_SIZE * N_HEADS, V_DIM),
                         cache_v[l].reshape(-1, PAGE_SIZE * N_HEADS, V_DIM),
                         page_table, lam_p, sub_g, lam_init=lam_init)
        wsp, bsp = _spatial_params(w_spatial[l], b_spatial[l], n_tok)
        hs = _mix_ffn(h1, o.reshape(rows_s, D_MODEL), sga, sgm, gu, vn_s, wsp, bsp, *tail, tm=rows_s, period=n_tok)

        kp_l.append(k_p.reshape(batch, seq, N_HEADS, V_DIM))
        vp_l.append(v_p.reshape(batch, seq, N_HEADS, V_DIM))
        ks_l.append(k_s.reshape(db, n_tok, N_HEADS, V_DIM))
        vs_l.append(v_s.reshape(db, n_tok, N_HEADS, V_DIM))
        gv_l.append(vn_s.reshape(db, n_tok, D_MODEL))
    return (hp.reshape(batch, seq, D_MODEL), hs.reshape(db, n_tok, D_MODEL),
            jnp.stack(kp_l), jnp.stack(vp_l), jnp.stack(ks_l), jnp.stack(vs_l), jnp.stack(gv_l))
```

```python
import functools
import math

import jax
import jax.numpy as jnp
import numpy as np
from jax import lax
from jax.experimental import pallas as pl
from jax.experimental.pallas import tpu as pltpu

D_MODEL = 1024
DEPTH = 1
PAGE_SIZE = 128
N_HEADS = 8
HEAD_DIM = 64
V_DIM = 2 * HEAD_DIM
CHUNK = 128
GROUP_WIDTH = 128
N_GROUPS = D_MODEL // GROUP_WIDTH
D_FF = 2816
ALPHA = (2 * DEPTH) ** 0.25
LN_EPS = 1e-5
NEG_INF = -1e30
QK_SCALE = HEAD_DIM ** -0.5
LOG2E = math.log2(math.e)
Q_PRESCALE = QK_SCALE * LOG2E

VMEM_LIMIT_BYTES = 56 * 1024 * 1024
ATTN_TQ = 256
ATTN_TK = 1024
HEADS_PER_STEP = 2
PAGES_PER_STEP = 8
PAGE_SLOTS = 3

F32 = jnp.float32
BF16 = jnp.bfloat16
BF16_EXACT_INT = 256
BF16_ROWS_PER_VREG = 16


def _bf16_terms(x, n):
    terms = []
    for _ in range(n):
        m, e = math.frexp(x)
        t = math.ldexp(round(m * BF16_EXACT_INT) / BF16_EXACT_INT, e)
        terms.append(t)
        x -= t
    return tuple(terms)


LOG2E_BF16_TERMS = _bf16_terms(LOG2E, 4)


def _slopes():
    return [2.0 ** (-8.0 * (h + 1) / N_HEADS) for h in range(N_HEADS)]


def _layer_norm(x, g, b):
    mu = jnp.mean(x, axis=-1, keepdims=True)
    xc = x - mu
    var = jnp.mean(xc * xc, axis=-1, keepdims=True)
    return xc * lax.rsqrt(var + LN_EPS) * g + b


def _gelu(x):
    return 0.5 * x * (1.0 + lax.erf(x * math.sqrt(0.5)))


def _swiglu(xb, wg_ref, wu_ref, wd_ref):
    gate = jnp.dot(xb, wg_ref[...], preferred_element_type=F32)
    up = jnp.dot(xb, wu_ref[...], preferred_element_type=F32)
    act = (gate * jax.nn.sigmoid(gate)) * up
    return jnp.dot(act.astype(BF16), wd_ref[...], preferred_element_type=F32)


def _diff_lambda(lam_ref, lam_init):
    lp = lam_ref[...]
    a = jnp.sum(lp[0:1] * lp[1:2], axis=-1, keepdims=True)
    b = jnp.sum(lp[2:3] * lp[3:4], axis=-1, keepdims=True)
    return jnp.exp(a) - jnp.exp(b) + lam_init


def _const_spec(shape):
    nd = len(shape)
    return pl.BlockSpec(shape, lambda *_: (0,) * nd, pipeline_mode=pl.Buffered(1))


def _params(n_axes):
    return pltpu.CompilerParams(
        dimension_semantics=("arbitrary",) * n_axes, vmem_limit_bytes=VMEM_LIMIT_BYTES)


def _ffn_ln_kernel(x_ref, wg_ref, wu_ref, wd_ref, g_ref, b_ref, o_ref):
    x = x_ref[...]
    y = _swiglu(x.astype(BF16), wg_ref, wu_ref, wd_ref)
    o_ref[...] = _layer_norm(ALPHA * x + 0.5 * y, g_ref[...], b_ref[...])


def _ffn_ln(x, wg, wu, wd, g, b, *, tm):
    rows = x.shape[0]
    row_spec = pl.BlockSpec((tm, D_MODEL), lambda i: (i, 0))
    return pl.pallas_call(
        _ffn_ln_kernel,
        grid=(rows // tm,),
        in_specs=[row_spec, _const_spec(wg.shape), _const_spec(wu.shape), _const_spec(wd.shape),
                  _const_spec(g.shape), _const_spec(b.shape)],
        out_specs=row_spec,
        out_shape=jax.ShapeDtypeStruct((rows, D_MODEL), F32),
        compiler_params=_params(1),
        name="ffn_ln",
    )(x, wg, wu, wd, g, b)


def _proj_kernel(h_ref, w_ref, lg_ref, lb_ref, *out_refs, attn_layouts):
    hb = h_ref[...].astype(BF16)

    def section(i):
        return jnp.dot(hb, w_ref[:, i * D_MODEL:(i + 1) * D_MODEL], preferred_element_type=F32)

    q = section(0) * Q_PRESCALE
    k = section(1)
    v = section(2)
    if attn_layouts:
        k_ref, v_ref, qt_ref, kb_ref, vt_ref, gu_ref, vn_ref, sga_ref, sgm_ref = out_refs
        qt_ref[0, 0] = q.T.astype(BF16)
        kb_ref[...] = k.astype(BF16)
        vt_ref[0, 0] = v.T.astype(BF16)
    else:
        q_ref, k_ref, v_ref, gu_ref, vn_ref, sga_ref, sgm_ref = out_refs
        q_ref[...] = q
    k_ref[...] = k
    v_ref[...] = v
    gu_ref[...] = _gelu(section(3))
    vn_ref[...] = _layer_norm(_gelu(section(4)), lg_ref[...], lb_ref[...])
    sga_ref[...] = jax.nn.sigmoid(section(5))
    sgm_ref[...] = jax.nn.sigmoid(section(6))


def _proj(h, w_in, ln_g, ln_b, *, tm, batch=None):
    rows = h.shape[0]
    n_tiles = rows // tm
    row_spec = pl.BlockSpec((tm, D_MODEL), lambda i: (i, 0))
    row_f32 = jax.ShapeDtypeStruct((rows, D_MODEL), F32)
    attn_layouts = batch is not None
    if attn_layouts:
        per_b = n_tiles // batch
        t_spec = pl.BlockSpec((1, 1, D_MODEL, tm), lambda i: (i // per_b, i % per_b, 0, 0))
        t_shape = jax.ShapeDtypeStruct((batch, per_b, D_MODEL, tm), BF16)
        out_specs = [row_spec, row_spec, t_spec, row_spec, t_spec] + [row_spec] * 4
        out_shape = [row_f32, row_f32, t_shape, jax.ShapeDtypeStruct((rows, D_MODEL), BF16), t_shape] + [row_f32] * 4
    else:
        out_specs = [row_spec] * 7
        out_shape = [row_f32] * 7
    return pl.pallas_call(
        functools.partial(_proj_kernel, attn_layouts=attn_layouts),
        grid=(n_tiles,),
        in_specs=[row_spec, _const_spec(w_in.shape), _const_spec(ln_g.shape), _const_spec(ln_b.shape)],
        out_specs=out_specs,
        out_shape=out_shape,
        compiler_params=_params(1),
        name="proj",
    )(h, w_in, ln_g, ln_b)


def _attn_prompt_kernel(qt_ref, k_ref, vt_ref, pos_ref, rel_ref, slope_ref, lam_ref, g_ref, o_ref,
                        w_ref, acc_ref, s_ref, *, lam_init):
    tq, tk, nh = ATTN_TQ, ATTN_TK, HEADS_PER_STEP
    sub = tk // tq
    qi = pl.program_id(2)
    n_full = qi // sub
    zeros = jnp.zeros((HEAD_DIM, tq), BF16)
    prow = lax.broadcasted_iota(jnp.int32, (V_DIM, 2 * tq), 0)
    pos_rows = jnp.zeros((V_DIM, 2 * tq), F32)
    for i, term in enumerate(LOG2E_BF16_TERMS * 2):
        pos_rows = jnp.where(prow == i, term, pos_rows)
    pos_rows = pos_rows.astype(BF16)
    for h in range(nh):
        qt = qt_ref[0, 0, h * V_DIM:(h + 1) * V_DIM, :]
        w_ref[h, 0:HEAD_DIM, 0:tq] = qt[0:HEAD_DIM]
        w_ref[h, HEAD_DIM:V_DIM, 0:tq] = zeros
        w_ref[h, 0:HEAD_DIM, tq:] = zeros
        w_ref[h, HEAD_DIM:V_DIM, tq:] = qt[HEAD_DIM:]
        w_ref[h, V_DIM:, :] = pos_rows

    def head_cols(h):
        return slice(h * V_DIM, (h + 1) * V_DIM)

    def with_ones(vt):
        return jnp.concatenate([vt, jnp.ones((BF16_ROWS_PER_VREG, vt.shape[1]), BF16)], axis=0)

    def absorb(j, src_ref, cmax, ml):
        out = []
        for h in range(nh):
            m, l = ml[2 * h], ml[2 * h + 1]
            d = slope_ref[h] * jnp.full((1, 2 * tq), j * tk - qi * tq, jnp.int32).astype(F32)
            m_new = jnp.maximum(m, cmax[h] + d)
            a = jnp.exp2(m - m_new)
            p = jnp.exp2(src_ref[h] - (m_new - d))
            vt = jnp.concatenate([vt_ref[0, j * sub + u, head_cols(h), :] for u in range(sub)], axis=1)
            pv = jnp.dot(with_ones(vt), p.astype(BF16), preferred_element_type=F32)
            acc_ref[h] = a * acc_ref[h] + pv[:V_DIM]
            out += [m_new, a * l + pv[V_DIM:V_DIM + 1]]
        return tuple(out)

    def diagonal(r):
        def f():
            nk = (r + 1) * tq
            cm = [None] * nh

            def scores0(u):
                rows = slice(u * tq, (u + 1) * tq)
                for h in range(nh):
                    lhs = jnp.concatenate([k_ref[0, 0, rows, head_cols(h)], pos_ref[h, rows]], axis=1)
                    s0 = jnp.dot(lhs, w_ref[h], preferred_element_type=F32)
                    s_ref[h, rows] = s0
                    bm = jnp.max(s0, axis=0, keepdims=True)
                    cm[h] = bm if cm[h] is None else jnp.maximum(cm[h], bm)

            s_d = []
            for h in range(nh):
                lhs = jnp.concatenate([k_ref[0, n_full, 0:nk, head_cols(h)], pos_ref[h, 0:nk]], axis=1)
                s_d.append(jnp.dot(lhs, w_ref[h], preferred_element_type=F32))
            scores0(0)
            ms, ds = [], []
            for h in range(nh):
                tail = jnp.where(rel_ref[...] <= 0.0, s_d[h][r * tq:], NEG_INF)
                s_d[h] = tail if r == 0 else jnp.concatenate([s_d[h][:r * tq], tail], axis=0)
                d = slope_ref[h] * float(-r * tq)
                ms.append(jnp.max(s_d[h], axis=0, keepdims=True) + d)
                ds.append(d)
            scores0(1)
            ps = [jnp.exp2(s_d[h] - (ms[h] - ds[h])) for h in range(nh)]
            scores0(2)
            out = []
            for h in range(nh):
                vt = jnp.concatenate([vt_ref[0, n_full * sub + u, head_cols(h), :] for u in range(r + 1)], axis=1)
                pv = jnp.dot(with_ones(vt), ps[h].astype(BF16), preferred_element_type=F32)
                acc_ref[h] = pv[:V_DIM]
                out += [ms[h], pv[V_DIM:V_DIM + 1]]
            scores0(3)
            return tuple(out) + tuple(cm)
        return f

    state = lax.switch(qi % sub, [diagonal(r) for r in range(sub)])

    def step(j, state):
        ml, cmax = state[:2 * nh], state[2 * nh:]
        out, cmax_next = [], []
        for h in range(nh):
            m, l = ml[2 * h], ml[2 * h + 1]
            d = slope_ref[h] * jnp.full((1, 2 * tq), j * tk - qi * tq, jnp.int32).astype(F32)
            m_new = jnp.maximum(m, cmax[h] + d)
            a = jnp.exp2(m - m_new)
            shift = m_new - d
            cm, lsum, pv = None, None, None
            for u in range(sub):
                rows = slice(u * tq, (u + 1) * tq)
                lhs = jnp.concatenate([k_ref[0, j + 1, rows, head_cols(h)], pos_ref[h, rows]], axis=1)
                s = jnp.dot(lhs, w_ref[h], preferred_element_type=F32)
                p = jnp.exp2(s_ref[h, rows] - shift)
                s_ref[h, rows] = s
                bm = jnp.max(s, axis=0, keepdims=True)
                cm = bm if cm is None else jnp.maximum(cm, bm)
                part = jnp.dot(with_ones(vt_ref[0, j * sub + u, head_cols(h), :]), p.astype(BF16),
                               preferred_element_type=F32)
                pv = part if pv is None else pv + part
            lsum = pv[V_DIM:V_DIM + 1]
            pv = pv[:V_DIM]
            acc_ref[h] = a * acc_ref[h] + pv
            out += [m_new, a * l + lsum]
            cmax_next.append(cm)
        return tuple(out) + tuple(cmax_next)

    state = lax.fori_loop(0, n_full - 1, step, state)
    ml, cmax = state[:2 * nh], state[2 * nh:]
    carry = lax.cond(n_full > 0, lambda: absorb(n_full - 1, s_ref, cmax, ml), lambda: ml)

    lam = _diff_lambda(lam_ref, lam_init)
    for h in range(nh):
        on = acc_ref[h] / carry[2 * h + 1]
        ot = on[:, 0:tq] - lam * on[:, tq:]
        ot = ot * lax.rsqrt(jnp.mean(ot * ot, axis=0, keepdims=True) + LN_EPS)
        o_ref[:, h * V_DIM:(h + 1) * V_DIM] = ot.T * g_ref[...] * (1.0 - lam_init)


def _attn_prompt(qt, kb, vt, lam_p, subln_g, *, lam_init):
    batch, nblk, _, tq = qt.shape
    tk, nh = ATTN_TK, HEADS_PER_STEP
    assert tq == ATTN_TQ and (nblk * tq) % tk == 0
    nchunk = nblk * tq // tk
    kb = kb.reshape(batch, nchunk, tk, D_MODEL)
    slopes = np.asarray(_slopes(), np.float32)
    c = np.arange(tk)
    pos = np.zeros((N_HEADS, tk, V_DIM), np.float32)
    n_terms = len(LOG2E_BF16_TERMS)
    for i in range(n_terms):
        pos[:, :, i] = slopes[:, None] * ((c // BF16_EXACT_INT) * BF16_EXACT_INT)[None]
        pos[:, :, n_terms + i] = slopes[:, None] * (c % BF16_EXACT_INT)[None]
    pos = jnp.asarray(pos, BF16)
    rel = (c[:tq, None] - np.tile(np.arange(tq), 2)[None, :]).astype(np.float32)
    slope_row = np.ascontiguousarray(np.broadcast_to((slopes * LOG2E)[:, None, None], (N_HEADS, 1, 2 * tq)))
    return pl.pallas_call(
        functools.partial(_attn_prompt_kernel, lam_init=lam_init),
        grid=(batch, N_HEADS // nh, nblk),
        in_specs=[
            pl.BlockSpec((1, 1, nh * V_DIM, tq), lambda b, h, i: (b, i, h, 0)),
            pl.BlockSpec((1, nchunk, tk, nh * V_DIM), lambda b, h, i: (b, 0, 0, h)),
            pl.BlockSpec((1, nblk, nh * V_DIM, tq), lambda b, h, i: (b, 0, h, 0)),
            pl.BlockSpec((nh, tk, V_DIM), lambda b, h, i: (h, 0, 0)),
            pl.BlockSpec(rel.shape, lambda b, h, i: (0, 0), pipeline_mode=pl.Buffered(1)),
            pl.BlockSpec((nh, 1, 2 * tq), lambda b, h, i: (h, 0, 0)),
            pl.BlockSpec(lam_p.shape, lambda b, h, i: (0, 0)),
            pl.BlockSpec(subln_g.shape, lambda b, h, i: (0, 0)),
        ],
        out_specs=pl.BlockSpec((tq, nh * V_DIM), lambda b, h, i: (b * nblk + i, h)),
        out_shape=jax.ShapeDtypeStruct((batch * nblk * tq, D_MODEL), F32),
        scratch_shapes=[pltpu.VMEM((nh, 2 * V_DIM, 2 * tq), BF16), pltpu.VMEM((nh, V_DIM, 2 * tq), F32),
                        pltpu.VMEM((nh, tk, 2 * tq), F32)],
        compiler_params=_params(3),
        name="attn_prompt",
    )(qt, kb, vt, pos, rel, slope_row, lam_p, subln_g)


def _attn_decode_kernel(pt_ref, q_ref, kn_ref, vn_ref, ck_hbm, cv_hbm, bias_ref, biasn_ref, slope_ref, lam_ref,
                        g_ref, o_ref, kbuf, vbuf, sem, qm_ref, m_ref, l_ref, acc_ref, *, lam_init, past_len, n_tok):
    g_pages = PAGES_PER_STEP
    step = pl.program_id(1)
    n_steps = pl.num_programs(1)
    rows = n_tok * N_HEADS
    t = pl.program_id(0) * n_steps + step
    t_end = pl.num_programs(0) * n_steps

    def page_copies(tt, slot):
        b, s = tt // n_steps, tt % n_steps
        copies = []
        for g in range(g_pages):
            page = pt_ref[b, s * g_pages + g]
            copies.append(pltpu.make_async_copy(ck_hbm.at[page], kbuf.at[slot, g], sem.at[slot]))
            copies.append(pltpu.make_async_copy(cv_hbm.at[page], vbuf.at[slot, g], sem.at[slot]))
        return copies

    @pl.when(t == 0)
    def _():
        for ahead in range(PAGE_SLOTS - 1):
            for c in page_copies(ahead, ahead):
                c.start()

    nxt = t + (PAGE_SLOTS - 1)

    @pl.when(nxt < t_end)
    def _():
        for c in page_copies(nxt, nxt % PAGE_SLOTS):
            c.start()

    slot = t % PAGE_SLOTS
    for c in page_copies(t, slot):
        c.wait()

    @pl.when(step == 0)
    def _():
        q = q_ref[0]
        lane = lax.broadcasted_iota(jnp.int32, q.shape, 1)
        qm_ref[0:rows, :] = jnp.where(lane < HEAD_DIM, q, 0.0).astype(BF16)
        qm_ref[rows:, :] = jnp.where(lane >= HEAD_DIM, q, 0.0).astype(BF16)
        m_ref[...] = jnp.full_like(m_ref, NEG_INF)
        l_ref[...] = jnp.zeros_like(l_ref)
        acc_ref[...] = jnp.zeros_like(acc_ref)

    def update(pages):
        qm = qm_ref[...]
        s_list, mx = [], None
        for kp, _, bias, d in pages:
            s = lax.dot_general(qm, kp, (((1,), (1,)), ((), ())), preferred_element_type=F32) + bias
            s_list.append(s)
            blk = jnp.max(s, axis=1, keepdims=True) + d
            mx = blk if mx is None else jnp.maximum(mx, blk)
        m = m_ref[...]
        m_new = jnp.maximum(m, mx)
        a = jnp.exp2(m - m_new)
        l = a * l_ref[...]
        acc = a * acc_ref[...]
        for s, (_, vp, _, d) in zip(s_list, pages):
            p = jnp.exp2(s - (m_new - d))
            l = l + jnp.sum(p, axis=1, keepdims=True)
            acc = acc + jnp.dot(p.astype(BF16), vp, preferred_element_type=F32)
        l_ref[...] = l
        acc_ref[...] = acc
        m_ref[...] = m_new

    slope = slope_ref[...]
    pages = []
    for g in range(g_pages):
        page = step * g_pages + g
        d = slope * jnp.full((2 * rows, 1), page * PAGE_SIZE - past_len, jnp.int32).astype(F32)
        pages.append((kbuf[slot, g].astype(BF16), vbuf[slot, g].astype(BF16), bias_ref[...], d))
    update(pages)

    @pl.when(step == pl.num_programs(1) - 1)
    def _():
        update([(kn_ref[0].astype(BF16), vn_ref[0].astype(BF16), biasn_ref[...], jnp.zeros((2 * rows, 1), F32))])
        lam = _diff_lambda(lam_ref, lam_init)
        on = acc_ref[...] / l_ref[...]
        o = on[0:rows] - lam * on[rows:]
        o = o * lax.rsqrt(jnp.mean(o * o, axis=-1, keepdims=True) + LN_EPS)
        o_ref[0] = o * g_ref[...] * (1.0 - lam_init)


def _attn_decode(q, k_new, v_new, cache_k, cache_v, page_table, lam_p, subln_g, *, lam_init):
    db, rows, _ = q.shape
    n_tok = rows // N_HEADS
    n_pages = page_table.shape[1]
    past_len = n_pages * PAGE_SIZE
    g_pages = PAGES_PER_STEP
    assert n_pages % g_pages == 0 and db * (n_pages // g_pages) >= PAGE_SLOTS - 1
    slopes = np.asarray(_slopes(), np.float32)
    row_h = np.tile(np.arange(N_HEADS), 2 * n_tok)
    row_t = np.tile(np.repeat(np.arange(n_tok), N_HEADS), 2)
    row_slope = (slopes * LOG2E)[row_h]
    col_pos = np.repeat(np.arange(PAGE_SIZE), N_HEADS)
    col_h = np.tile(np.arange(N_HEADS), PAGE_SIZE)
    bias = np.where(row_h[:, None] == col_h[None, :],
                    row_slope[:, None] * (col_pos[None, :] - row_t[:, None]), NEG_INF).astype(np.float32)
    ncol_t = np.repeat(np.arange(n_tok), N_HEADS)
    ncol_h = np.tile(np.arange(N_HEADS), n_tok)
    dist = row_t[:, None] - ncol_t[None, :]
    bias_new = np.where((row_h[:, None] == ncol_h[None, :]) & (dist >= 0),
                        -row_slope[:, None] * dist, NEG_INF).astype(np.float32)
    slope_col = np.ascontiguousarray(row_slope[:, None])

    tok_spec = pl.BlockSpec((1, rows, V_DIM), lambda b, s, pt: (b, 0, 0))

    def const2(shape):
        return pl.BlockSpec(shape, lambda b, s, pt: (0, 0))

    hbm = pl.BlockSpec(memory_space=pl.ANY)
    page_block = (PAGE_SLOTS, g_pages, PAGE_SIZE * N_HEADS, V_DIM)
    grid_spec = pltpu.PrefetchScalarGridSpec(
        num_scalar_prefetch=1,
        grid=(db, n_pages // g_pages),
        in_specs=[tok_spec, tok_spec, tok_spec, hbm, hbm,
                  const2(bias.shape), const2(bias_new.shape), const2(slope_col.shape),
                  const2(lam_p.shape), const2(subln_g.shape)],
        out_specs=tok_spec,
        scratch_shapes=[pltpu.VMEM(page_block, F32), pltpu.VMEM(page_block, F32),
                        pltpu.SemaphoreType.DMA((PAGE_SLOTS,)),
                        pltpu.VMEM((2 * rows, V_DIM), BF16), pltpu.VMEM((2 * rows, 1), F32),
                        pltpu.VMEM((2 * rows, 1), F32), pltpu.VMEM((2 * rows, V_DIM), F32)],
    )
    return pl.pallas_call(
        functools.partial(_attn_decode_kernel, lam_init=lam_init, past_len=past_len, n_tok=n_tok),
        grid_spec=grid_spec,
        out_shape=jax.ShapeDtypeStruct((db, rows, V_DIM), F32),
        compiler_params=_params(2),
        name="attn_decode",
    )(page_table, q, k_new, v_new, cache_k, cache_v, bias, bias_new, slope_col, lam_p, subln_g)


def _mix_ffn_kernel(h_ref, o_ref, sga_ref, sgm_ref, gu_ref, vn_ref, wsp_ref, bsp_ref, wo_ref,
                    g2_ref, b2_ref, wg_ref, wu_ref, wd_ref, g3_ref, b3_ref, y_ref, s_ref, *, period):
    tm = h_ref.shape[0]
    r = lax.broadcasted_iota(jnp.int32, (CHUNK, CHUNK), 0)
    c = lax.broadcasted_iota(jnp.int32, (CHUNK, CHUNK), 1)
    keep = (r // period == c // period) & (c <= r)
    for grp in range(N_GROUPS):
        w = jnp.where(keep, wsp_ref[grp], 0.0).astype(BF16)
        cols = slice(grp * GROUP_WIDTH, (grp + 1) * GROUP_WIDTH)
        for ch in range(tm // CHUNK):
            rws = slice(ch * CHUNK, (ch + 1) * CHUNK)
            f = jnp.dot(w, vn_ref[rws, cols].astype(BF16), preferred_element_type=F32) + bsp_ref[grp]
            s_ref[rws, cols] = gu_ref[rws, cols] * f
    mix_in = sga_ref[...] * o_ref[...] + sgm_ref[...] * s_ref[...]
    mix = jnp.dot(mix_in.astype(BF16), wo_ref[...], preferred_element_type=F32)
    h2 = _layer_norm(ALPHA * h_ref[...] + mix, g2_ref[...], b2_ref[...])
    y = _swiglu(h2.astype(BF16), wg_ref, wu_ref, wd_ref)
    y_ref[...] = _layer_norm(ALPHA * h2 + 0.5 * y, g3_ref[...], b3_ref[...])


def _mix_ffn(h, o, sga, sgm, gu, vn, wsp, bsp, wo, g2, b2, wg, wu, wd, g3, b3, *, tm, period):
    rows = h.shape[0]
    row_spec = pl.BlockSpec((tm, D_MODEL), lambda i: (i, 0))
    consts = (wsp, bsp, wo, g2, b2, wg, wu, wd, g3, b3)
    return pl.pallas_call(
        functools.partial(_mix_ffn_kernel, period=period),
        grid=(rows // tm,),
        in_specs=[row_spec] * 6 + [_const_spec(a.shape) for a in consts],
        out_specs=row_spec,
        out_shape=jax.ShapeDtypeStruct((rows, D_MODEL), F32),
        scratch_shapes=[pltpu.VMEM((tm, D_MODEL), F32)],
        compiler_params=_params(1),
        name="mix_ffn",
    )(h, o, sga, sgm, gu, vn, *consts)


def _spatial_params(w_spatial, b_spatial, period):
    if period == CHUNK:
        w, b = w_spatial, b_spatial
    else:
        sel = np.equal.outer(np.arange(CHUNK) % period, np.arange(period)).astype(np.float32)
        hi = lax.Precision.HIGHEST
        w = jnp.einsum("ra,gab,cb->grc", sel, w_spatial[:, :period, :period], sel, precision=hi)
        b = jnp.einsum("ra,ga->gr", sel, b_spatial[:, :period], precision=hi)
    return w, b[:, :, None]


def kernel(x_prompt, x_sample, cache_k, cache_v, page_table, ln1_g, ln1_b, ffn1_wg, ffn1_wu, ffn1_wd, w_in, lam_q1, lam_k1, lam_q2, lam_k2, subln_g, gmlp_ln_g, gmlp_ln_b, w_spatial, b_spatial, w_out, ln2_g, ln2_b, ffn2_wg, ffn2_wu, ffn2_wd, ln3_g, ln3_b):
    batch, seq, _ = x_prompt.shape
    db, n_tok, _ = x_sample.shape
    rows_p, rows_s = batch * seq, db * n_tok
    assert rows_s == CHUNK and CHUNK % n_tok == 0 and seq % ATTN_TK == 0
    hp = x_prompt.reshape(rows_p, D_MODEL)
    hs = x_sample.reshape(rows_s, D_MODEL)
    row = lambda a: a.reshape(1, -1)
    tm = ATTN_TQ
    kp_l, vp_l, ks_l, vs_l, gv_l = [], [], [], [], []
    for l in range(DEPTH):
        lam_init = 0.8 - 0.6 * math.exp(-0.3 * l)
        wg1, wu1, wd1 = ffn1_wg[l].astype(BF16), ffn1_wu[l].astype(BF16), ffn1_wd[l].astype(BF16)
        wg2, wu2, wd2 = ffn2_wg[l].astype(BF16), ffn2_wu[l].astype(BF16), ffn2_wd[l].astype(BF16)
        w_in_b, w_out_b = w_in[l].astype(BF16), w_out[l].astype(BF16)
        lam_p = jnp.stack([lam_q1[l], lam_k1[l], lam_q2[l], lam_k2[l]])
        sub_g = row(subln_g[l])
        ffn1 = (wg1, wu1, wd1, row(ln1_g[l]), row(ln1_b[l]))
        tail = (w_out_b, row(ln2_g[l]), row(ln2_b[l]), wg2, wu2, wd2, row(ln3_g[l]), row(ln3_b[l]))
        gln = (row(gmlp_ln_g[l]), row(gmlp_ln_b[l]))

        h1 = _ffn_ln(hp, *ffn1, tm=tm)
        k_p, v_p, qt, kb, vt, gu, vn, sga, sgm = _proj(h1, w_in_b, *gln, tm=tm, batch=batch)
        o = _attn_prompt(qt, kb, vt, lam_p, sub_g, lam_init=lam_init)
        wsp, bsp = _spatial_params(w_spatial[l], b_spatial[l], CHUNK)
        hp = _mix_ffn(h1, o, sga, sgm, gu, vn, wsp, bsp, *tail, tm=tm, period=CHUNK)

        h1 = _ffn_ln(hs, *ffn1, tm=rows_s)
        q_s, k_s, v_s, gu, vn_s, sga, sgm = _proj(h1, w_in_b, *gln, tm=rows_s)
        tok = lambda a: a.reshape(db, n_tok * N_HEADS, V_DIM)
        o = _attn_decode(tok(q_s), tok(k_s), tok(v_s),
                         cache_k[l].reshape(-1, PAGE_SIZE * N_HEADS, V_DIM),
                         cache_v[l].reshape(-1, PAGE_SIZE * N_HEADS, V_DIM),
                         page_table, lam_p, sub_g, lam_init=lam_init)
        wsp, bsp = _spatial_params(w_spatial[l], b_spatial[l], n_tok)
        hs = _mix_ffn(h1, o.reshape(rows_s, D_MODEL), sga, sgm, gu, vn_s, wsp, bsp, *tail, tm=rows_s, period=n_tok)

        kp_l.append(k_p.reshape(batch, seq, N_HEADS, V_DIM))
        vp_l.append(v_p.reshape(batch, seq, N_HEADS, V_DIM))
        ks_l.append(k_s.reshape(db, n_tok, N_HEADS, V_DIM))
        vs_l.append(v_s.reshape(db, n_tok, N_HEADS, V_DIM))
        gv_l.append(vn_s.reshape(db, n_tok, D_MODEL))
    return (hp.reshape(batch, seq, D_MODEL), hs.reshape(db, n_tok, D_MODEL),
            jnp.stack(kp_l), jnp.stack(vp_l), jnp.stack(ks_l), jnp.stack(vs_l), jnp.stack(gv_l))
```

```python
import functools
import math

import jax
import jax.numpy as jnp
import numpy as np
from jax import lax
from jax.experimental import pallas as pl
from jax.experimental.pallas import tpu as pltpu

D_MODEL = 1024
DEPTH = 1
PAGE_SIZE = 128
N_HEADS = 8
HEAD_DIM = 64
V_DIM = 2 * HEAD_DIM
CHUNK = 128
GROUP_WIDTH = 128
N_GROUPS = D_MODEL // GROUP_WIDTH
D_FF = 2816
ALPHA = (2 * DEPTH) ** 0.25
LN_EPS = 1e-5
NEG_INF = -1e30
QK_SCALE = HEAD_DIM ** -0.5
LOG2E = math.log2(math.e)
Q_PRESCALE = QK_SCALE * LOG2E

VMEM_LIMIT_BYTES = 56 * 1024 * 1024
ATTN_TQ = 256
ATTN_TK = 1024
HEADS_PER_STEP = 2
PAGES_PER_STEP = 8
PAGE_SLOTS = 3

F32 = jnp.float32
BF16 = jnp.bfloat16
BF16_EXACT_INT = 256
BF16_ROWS_PER_VREG = 16


def _bf16_terms(x, n):
    terms = []
    for _ in range(n):
        m, e = math.frexp(x)
        t = math.ldexp(round(m * BF16_EXACT_INT) / BF16_EXACT_INT, e)
        terms.append(t)
        x -= t
    return tuple(terms)


LOG2E_BF16_TERMS = _bf16_terms(LOG2E, 4)


def _slopes():
    return [2.0 ** (-8.0 * (h + 1) / N_HEADS) for h in range(N_HEADS)]


def _layer_norm(x, g, b):
    mu = jnp.mean(x, axis=-1, keepdims=True)
    xc = x - mu
    var = jnp.mean(xc * xc, axis=-1, keepdims=True)
    return xc * lax.rsqrt(var + LN_EPS) * g + b


def _gelu(x):
    return 0.5 * x * (1.0 + lax.erf(x * math.sqrt(0.5)))


def _swiglu(xb, wg_ref, wu_ref, wd_ref):
    gate = jnp.dot(xb, wg_ref[...], preferred_element_type=F32)
    up = jnp.dot(xb, wu_ref[...], preferred_element_type=F32)
    act = (gate * jax.nn.sigmoid(gate)) * up
    return jnp.dot(act.astype(BF16), wd_ref[...], preferred_element_type=F32)


def _diff_lambda(lam_ref, lam_init):
    lp = lam_ref[...]
    a = jnp.sum(lp[0:1] * lp[1:2], axis=-1, keepdims=True)
    b = jnp.sum(lp[2:3] * lp[3:4], axis=-1, keepdims=True)
    return jnp.exp(a) - jnp.exp(b) + lam_init


def _const_spec(shape):
    nd = len(shape)
    return pl.BlockSpec(shape, lambda *_: (0,) * nd, pipeline_mode=pl.Buffered(1))


def _params(n_axes):
    return pltpu.CompilerParams(
        dimension_semantics=("arbitrary",) * n_axes, vmem_limit_bytes=VMEM_LIMIT_BYTES)


def _ffn_ln_kernel(x_ref, wg_ref, wu_ref, wd_ref, g_ref, b_ref, o_ref):
    x = x_ref[...]
    y = _swiglu(x.astype(BF16), wg_ref, wu_ref, wd_ref)
    o_ref[...] = _layer_norm(ALPHA * x + 0.5 * y, g_ref[...], b_ref[...])


def _ffn_ln(x, wg, wu, wd, g, b, *, tm):
    rows = x.shape[0]
    row_spec = pl.BlockSpec((tm, D_MODEL), lambda i: (i, 0))
    return pl.pallas_call(
        _ffn_ln_kernel,
        grid=(rows // tm,),
        in_specs=[row_spec, _const_spec(wg.shape), _const_spec(wu.shape), _const_spec(wd.shape),
                  _const_spec(g.shape), _const_spec(b.shape)],
        out_specs=row_spec,
        out_shape=jax.ShapeDtypeStruct((rows, D_MODEL), F32),
        compiler_params=_params(1),
        name="ffn_ln",
    )(x, wg, wu, wd, g, b)


def _proj_kernel(h_ref, w_ref, lg_ref, lb_ref, *out_refs, attn_layouts):
    hb = h_ref[...].astype(BF16)

    def section(i):
        return jnp.dot(hb, w_ref[:, i * D_MODEL:(i + 1) * D_MODEL], preferred_element_type=F32)

    q = section(0) * Q_PRESCALE
    k = section(1)
    v = section(2)
    if attn_layouts:
        k_ref, v_ref, qt_ref, kb_ref, vt_ref, gu_ref, vn_ref, sga_ref, sgm_ref = out_refs
        qt_ref[0, 0] = q.T.astype(BF16)
        kb_ref[...] = k.astype(BF16)
        vt_ref[0, 0] = v.T.astype(BF16)
    else:
        q_ref, k_ref, v_ref, gu_ref, vn_ref, sga_ref, sgm_ref = out_refs
        q_ref[...] = q
    k_ref[...] = k
    v_ref[...] = v
    gu_ref[...] = _gelu(section(3))
    vn_ref[...] = _layer_norm(_gelu(section(4)), lg_ref[...], lb_ref[...])
    sga_ref[...] = jax.nn.sigmoid(section(5))
    sgm_ref[...] = jax.nn.sigmoid(section(6))


def _proj(h, w_in, ln_g, ln_b, *, tm, batch=None):
    rows = h.shape[0]
    n_tiles = rows // tm
    row_spec = pl.BlockSpec((tm, D_MODEL), lambda i: (i, 0))
    row_f32 = jax.ShapeDtypeStruct((rows, D_MODEL), F32)
    attn_layouts = batch is not None
    if attn_layouts:
        per_b = n_tiles // batch
        t_spec = pl.BlockSpec((1, 1, D_MODEL, tm), lambda i: (i // per_b, i % per_b, 0, 0))
        t_shape = jax.ShapeDtypeStruct((batch, per_b, D_MODEL, tm), BF16)
        out_specs = [row_spec, row_spec, t_spec, row_spec, t_spec] + [row_spec] * 4
        out_shape = [row_f32, row_f32, t_shape, jax.ShapeDtypeStruct((rows, D_MODEL), BF16), t_shape] + [row_f32] * 4
    else:
        out_specs = [row_spec] * 7
        out_shape = [row_f32] * 7
    return pl.pallas_call(
        functools.partial(_proj_kernel, attn_layouts=attn_layouts),
        grid=(n_tiles,),
        in_specs=[row_spec, _const_spec(w_in.shape), _const_spec(ln_g.shape), _const_spec(ln_b.shape)],
        out_specs=out_specs,
        out_shape=out_shape,
        compiler_params=_params(1),
        name="proj",
    )(h, w_in, ln_g, ln_b)


def _attn_prompt_kernel(qt_ref, k_ref, vt_ref, pos_ref, rel_ref, slope_ref, lam_ref, g_ref, o_ref,
                        w_ref, acc_ref, s_ref, *, lam_init):
    tq, tk, nh = ATTN_TQ, ATTN_TK, HEADS_PER_STEP
    sub = tk // tq
    qi = pl.program_id(2)
    n_full = qi // sub
    zeros = jnp.zeros((HEAD_DIM, tq), BF16)
    prow = lax.broadcasted_iota(jnp.int32, (V_DIM, 2 * tq), 0)
    pos_rows = jnp.zeros((V_DIM, 2 * tq), F32)
    for i, term in enumerate(LOG2E_BF16_TERMS * 2):
        pos_rows = jnp.where(prow == i, term, pos_rows)
    pos_rows = pos_rows.astype(BF16)
    for h in range(nh):
        qt = qt_ref[0, 0, h * V_DIM:(h + 1) * V_DIM, :]
        w_ref[h, 0:HEAD_DIM, 0:tq] = qt[0:HEAD_DIM]
        w_ref[h, HEAD_DIM:V_DIM, 0:tq] = zeros
        w_ref[h, 0:HEAD_DIM, tq:] = zeros
        w_ref[h, HEAD_DIM:V_DIM, tq:] = qt[HEAD_DIM:]
        w_ref[h, V_DIM:, :] = pos_rows

    def head_cols(h):
        return slice(h * V_DIM, (h + 1) * V_DIM)

    def with_ones(vt):
        return jnp.concatenate([vt, jnp.ones((BF16_ROWS_PER_VREG, vt.shape[1]), BF16)], axis=0)

    def absorb(j, src_ref, cmax, ml):
        out = []
        for h in range(nh):
            m, l = ml[2 * h], ml[2 * h + 1]
            d = slope_ref[h] * jnp.full((1, 2 * tq), j * tk - qi * tq, jnp.int32).astype(F32)
            m_new = jnp.maximum(m, cmax[h] + d)
            a = jnp.exp2(m - m_new)
            p = jnp.exp2(src_ref[h] - (m_new - d))
            vt = jnp.concatenate([vt_ref[0, j * sub + u, head_cols(h), :] for u in range(sub)], axis=1)
            pv = jnp.dot(with_ones(vt), p.astype(BF16), preferred_element_type=F32)
            acc_ref[h] = a * acc_ref[h] + pv[:V_DIM]
            out += [m_new, a * l + pv[V_DIM:V_DIM + 1]]
        return tuple(out)

    def diagonal(r, with_scores0):
        def f():
            nk = (r + 1) * tq
            cm = [None if with_scores0 else jnp.full((1, 2 * tq), NEG_INF, F32)] * nh

            def scores0(u):
                if not with_scores0:
                    return
                rows = slice(u * tq, (u + 1) * tq)
                for h in range(nh):
                    lhs = jnp.concatenate([k_ref[0, 0, rows, head_cols(h)], pos_ref[h, rows]], axis=1)
                    s0 = jnp.dot(lhs, w_ref[h], preferred_element_type=F32)
                    s_ref[h, rows] = s0
                    bm = jnp.max(s0, axis=0, keepdims=True)
                    cm[h] = bm if cm[h] is None else jnp.maximum(cm[h], bm)

            s_d = []
            for h in range(nh):
                lhs = jnp.concatenate([k_ref[0, n_full, 0:nk, head_cols(h)], pos_ref[h, 0:nk]], axis=1)
                s_d.append(jnp.dot(lhs, w_ref[h], preferred_element_type=F32))
            scores0(0)
            ms, ds = [], []
            for h in range(nh):
                tail = jnp.where(rel_ref[...] <= 0.0, s_d[h][r * tq:], NEG_INF)
                s_d[h] = tail if r == 0 else jnp.concatenate([s_d[h][:r * tq], tail], axis=0)
                d = slope_ref[h] * float(-r * tq)
                ms.append(jnp.max(s_d[h], axis=0, keepdims=True) + d)
                ds.append(d)
            scores0(1)
            ps = [jnp.exp2(s_d[h] - (ms[h] - ds[h])) for h in range(nh)]
            scores0(2)
            out = []
            for h in range(nh):
                vt = jnp.concatenate([vt_ref[0, n_full * sub + u, head_cols(h), :] for u in range(r + 1)], axis=1)
                pv = jnp.dot(with_ones(vt), ps[h].astype(BF16), preferred_element_type=F32)
                acc_ref[h] = pv[:V_DIM]
                out += [ms[h], pv[V_DIM:V_DIM + 1]]
            scores0(3)
            return tuple(out) + tuple(cm)
        return f

    branches = [diagonal(r, False) for r in range(sub)] + [diagonal(r, True) for r in range(sub)]
    state = lax.switch(qi % sub + sub * (n_full > 0).astype(jnp.int32), branches)

    def step(j, state):
        ml, cmax = state[:2 * nh], state[2 * nh:]
        out, cmax_next = [], []
        for h in range(nh):
            m, l = ml[2 * h], ml[2 * h + 1]
            d = slope_ref[h] * jnp.full((1, 2 * tq), j * tk - qi * tq, jnp.int32).astype(F32)
            m_new = jnp.maximum(m, cmax[h] + d)
            a = jnp.exp2(m - m_new)
            shift = m_new - d
            cm, lsum, pv = None, None, None
            for u in range(sub):
                rows = slice(u * tq, (u + 1) * tq)
                lhs = jnp.concatenate([k_ref[0, j + 1, rows, head_cols(h)], pos_ref[h, rows]], axis=1)
                s = jnp.dot(lhs, w_ref[h], preferred_element_type=F32)
                p = jnp.exp2(s_ref[h, rows] - shift)
                s_ref[h, rows] = s
                bm = jnp.max(s, axis=0, keepdims=True)
                cm = bm if cm is None else jnp.maximum(cm, bm)
                part = jnp.dot(with_ones(vt_ref[0, j * sub + u, head_cols(h), :]), p.astype(BF16),
                               preferred_element_type=F32)
                pv = part if pv is None else pv + part
            lsum = pv[V_DIM:V_DIM + 1]
            pv = pv[:V_DIM]
            acc_ref[h] = a * acc_ref[h] + pv
            out += [m_new, a * l + lsum]
            cmax_next.append(cm)
        return tuple(out) + tuple(cmax_next)

    state = lax.fori_loop(0, n_full - 1, step, state)
    ml, cmax = state[:2 * nh], state[2 * nh:]
    carry = lax.cond(n_full > 0, lambda: absorb(n_full - 1, s_ref, cmax, ml), lambda: ml)

    lam = _diff_lambda(lam_ref, lam_init)
    for h in range(nh):
        on = acc_ref[h] / carry[2 * h + 1]
        ot = on[:, 0:tq] - lam * on[:, tq:]
        ot = ot * lax.rsqrt(jnp.mean(ot * ot, axis=0, keepdims=True) + LN_EPS)
        o_ref[:, h * V_DIM:(h + 1) * V_DIM] = ot.T * g_ref[...] * (1.0 - lam_init)


def _attn_prompt(qt, kb, vt, lam_p, subln_g, *, lam_init):
    batch, nblk, _, tq = qt.shape
    tk, nh = ATTN_TK, HEADS_PER_STEP
    assert tq == ATTN_TQ and (nblk * tq) % tk == 0
    nchunk = nblk * tq // tk
    kb = kb.reshape(batch, nchunk, tk, D_MODEL)
    slopes = np.asarray(_slopes(), np.float32)
    c = np.arange(tk)
    pos = np.zeros((N_HEADS, tk, V_DIM), np.float32)
    n_terms = len(LOG2E_BF16_TERMS)
    for i in range(n_terms):
        pos[:, :, i] = slopes[:, None] * ((c // BF16_EXACT_INT) * BF16_EXACT_INT)[None]
        pos[:, :, n_terms + i] = slopes[:, None] * (c % BF16_EXACT_INT)[None]
    pos = jnp.asarray(pos, BF16)
    rel = (c[:tq, None] - np.tile(np.arange(tq), 2)[None, :]).astype(np.float32)
    slope_row = np.ascontiguousarray(np.broadcast_to((slopes * LOG2E)[:, None, None], (N_HEADS, 1, 2 * tq)))
    return pl.pallas_call(
        functools.partial(_attn_prompt_kernel, lam_init=lam_init),
        grid=(batch, N_HEADS // nh, nblk),
        in_specs=[
            pl.BlockSpec((1, 1, nh * V_DIM, tq), lambda b, h, i: (b, i, h, 0)),
            pl.BlockSpec((1, nchunk, tk, nh * V_DIM), lambda b, h, i: (b, 0, 0, h)),
            pl.BlockSpec((1, nblk, nh * V_DIM, tq), lambda b, h, i: (b, 0, h, 0)),
            pl.BlockSpec((nh, tk, V_DIM), lambda b, h, i: (h, 0, 0)),
            pl.BlockSpec(rel.shape, lambda b, h, i: (0, 0), pipeline_mode=pl.Buffered(1)),
            pl.BlockSpec((nh, 1, 2 * tq), lambda b, h, i: (h, 0, 0)),
            pl.BlockSpec(lam_p.shape, lambda b, h, i: (0, 0)),
            pl.BlockSpec(subln_g.shape, lambda b, h, i: (0, 0)),
        ],
        out_specs=pl.BlockSpec((tq, nh * V_DIM), lambda b, h, i: (b * nblk + i, h)),
        out_shape=jax.ShapeDtypeStruct((batch * nblk * tq, D_MODEL), F32),
        scratch_shapes=[pltpu.VMEM((nh, 2 * V_DIM, 2 * tq), BF16), pltpu.VMEM((nh, V_DIM, 2 * tq), F32),
                        pltpu.VMEM((nh, tk, 2 * tq), F32)],
        compiler_params=_params(3),
        name="attn_prompt",
    )(qt, kb, vt, pos, rel, slope_row, lam_p, subln_g)


def _attn_decode_kernel(pt_ref, q_ref, kn_ref, vn_ref, ck_hbm, cv_hbm, bias_ref, biasn_ref, slope_ref, lam_ref,
                        g_ref, o_ref, kbuf, vbuf, sem, qm_ref, m_ref, l_ref, acc_ref, *, lam_init, past_len, n_tok):
    g_pages = PAGES_PER_STEP
    step = pl.program_id(1)
    n_steps = pl.num_programs(1)
    rows = n_tok * N_HEADS
    t = pl.program_id(0) * n_steps + step
    t_end = pl.num_programs(0) * n_steps

    def page_copies(tt, slot):
        b, s = tt // n_steps, tt % n_steps
        copies = []
        for g in range(g_pages):
            page = pt_ref[b, s * g_pages + g]
            copies.append(pltpu.make_async_copy(ck_hbm.at[page], kbuf.at[slot, g], sem.at[slot]))
            copies.append(pltpu.make_async_copy(cv_hbm.at[page], vbuf.at[slot, g], sem.at[slot]))
        return copies

    @pl.when(t == 0)
    def _():
        for ahead in range(PAGE_SLOTS - 1):
            for c in page_copies(ahead, ahead):
                c.start()

    nxt = t + (PAGE_SLOTS - 1)

    @pl.when(nxt < t_end)
    def _():
        for c in page_copies(nxt, nxt % PAGE_SLOTS):
            c.start()

    slot = t % PAGE_SLOTS
    for c in page_copies(t, slot):
        c.wait()

    @pl.when(step == 0)
    def _():
        q = q_ref[0]
        lane = lax.broadcasted_iota(jnp.int32, q.shape, 1)
        qm_ref[0:rows, :] = jnp.where(lane < HEAD_DIM, q, 0.0).astype(BF16)
        qm_ref[rows:, :] = jnp.where(lane >= HEAD_DIM, q, 0.0).astype(BF16)
        m_ref[...] = jnp.full_like(m_ref, NEG_INF)
        l_ref[...] = jnp.zeros_like(l_ref)
        acc_ref[...] = jnp.zeros_like(acc_ref)

    def update(pages):
        qm = qm_ref[...]
        s_list, mx = [], None
        for kp, _, bias, d in pages:
            s = lax.dot_general(qm, kp, (((1,), (1,)), ((), ())), preferred_element_type=F32) + bias
            s_list.append(s)
            blk = jnp.max(s, axis=1, keepdims=True) + d
            mx = blk if mx is None else jnp.maximum(mx, blk)
        m = m_ref[...]
        m_new = jnp.maximum(m, mx)
        a = jnp.exp2(m - m_new)
        l = a * l_ref[...]
        acc = a * acc_ref[...]
        for s, (_, vp, _, d) in zip(s_list, pages):
            p = jnp.exp2(s - (m_new - d))
            l = l + jnp.sum(p, axis=1, keepdims=True)
            acc = acc + jnp.dot(p.astype(BF16), vp, preferred_element_type=F32)
        l_ref[...] = l
        acc_ref[...] = acc
        m_ref[...] = m_new

    slope = slope_ref[...]
    pages = []
    for g in range(g_pages):
        page = step * g_pages + g
        d = slope * jnp.full((2 * rows, 1), page * PAGE_SIZE - past_len, jnp.int32).astype(F32)
        pages.append((kbuf[slot, g].astype(BF16), vbuf[slot, g].astype(BF16), bias_ref[...], d))
    update(pages)

    @pl.when(step == pl.num_programs(1) - 1)
    def _():
        update([(kn_ref[0].astype(BF16), vn_ref[0].astype(BF16), biasn_ref[...], jnp.zeros((2 * rows, 1), F32))])
        lam = _diff_lambda(lam_ref, lam_init)
        on = acc_ref[...] / l_ref[...]
        o = on[0:rows] - lam * on[rows:]
        o = o * lax.rsqrt(jnp.mean(o * o, axis=-1, keepdims=True) + LN_EPS)
        o_ref[0] = o * g_ref[...] * (1.0 - lam_init)


def _attn_decode(q, k_new, v_new, cache_k, cache_v, page_table, lam_p, subln_g, *, lam_init):
    db, rows, _ = q.shape
    n_tok = rows // N_HEADS
    n_pages = page_table.shape[1]
    past_len = n_pages * PAGE_SIZE
    g_pages = PAGES_PER_STEP
    assert n_pages % g_pages == 0 and db * (n_pages // g_pages) >= PAGE_SLOTS - 1
    slopes = np.asarray(_slopes(), np.float32)
    row_h = np.tile(np.arange(N_HEADS), 2 * n_tok)
    row_t = np.tile(np.repeat(np.arange(n_tok), N_HEADS), 2)
    row_slope = (slopes * LOG2E)[row_h]
    col_pos = np.repeat(np.arange(PAGE_SIZE), N_HEADS)
    col_h = np.tile(np.arange(N_HEADS), PAGE_SIZE)
    bias = np.where(row_h[:, None] == col_h[None, :],
                    row_slope[:, None] * (col_pos[None, :] - row_t[:, None]), NEG_INF).astype(np.float32)
    ncol_t = np.repeat(np.arange(n_tok), N_HEADS)
    ncol_h = np.tile(np.arange(N_HEADS), n_tok)
    dist = row_t[:, None] - ncol_t[None, :]
    bias_new = np.where((row_h[:, None] == ncol_h[None, :]) & (dist >= 0),
                        -row_slope[:, None] * dist, NEG_INF).astype(np.float32)
    slope_col = np.ascontiguousarray(row_slope[:, None])

    tok_spec = pl.BlockSpec((1, rows, V_DIM), lambda b, s, pt: (b, 0, 0))

    def const2(shape):
        return pl.BlockSpec(shape, lambda b, s, pt: (0, 0))

    hbm = pl.BlockSpec(memory_space=pl.ANY)
    page_block = (PAGE_SLOTS, g_pages, PAGE_SIZE * N_HEADS, V_DIM)
    grid_spec = pltpu.PrefetchScalarGridSpec(
        num_scalar_prefetch=1,
        grid=(db, n_pages // g_pages),
        in_specs=[tok_spec, tok_spec, tok_spec, hbm, hbm,
                  const2(bias.shape), const2(bias_new.shape), const2(slope_col.shape),
                  const2(lam_p.shape), const2(subln_g.shape)],
        out_specs=tok_spec,
        scratch_shapes=[pltpu.VMEM(page_block, F32), pltpu.VMEM(page_block, F32),
                        pltpu.SemaphoreType.DMA((PAGE_SLOTS,)),
                        pltpu.VMEM((2 * rows, V_DIM), BF16), pltpu.VMEM((2 * rows, 1), F32),
                        pltpu.VMEM((2 * rows, 1), F32), pltpu.VMEM((2 * rows, V_DIM), F32)],
    )
    return pl.pallas_call(
        functools.partial(_attn_decode_kernel, lam_init=lam_init, past_len=past_len, n_tok=n_tok),
        grid_spec=grid_spec,
        out_shape=jax.ShapeDtypeStruct((db, rows, V_DIM), F32),
        compiler_params=_params(2),
        name="attn_decode",
    )(page_table, q, k_new, v_new, cache_k, cache_v, bias, bias_new, slope_col, lam_p, subln_g)


def _mix_ffn_kernel(h_ref, o_ref, sga_ref, sgm_ref, gu_ref, vn_ref, wsp_ref, bsp_ref, wo_ref,
                    g2_ref, b2_ref, wg_ref, wu_ref, wd_ref, g3_ref, b3_ref, y_ref, s_ref, *, period):
    tm = h_ref.shape[0]
    r = lax.broadcasted_iota(jnp.int32, (CHUNK, CHUNK), 0)
    c = lax.broadcasted_iota(jnp.int32, (CHUNK, CHUNK), 1)
    keep = (r // period == c // period) & (c <= r)
    for grp in range(N_GROUPS):
        w = jnp.where(keep, wsp_ref[grp], 0.0).astype(BF16)
        cols = slice(grp * GROUP_WIDTH, (grp + 1) * GROUP_WIDTH)
        for ch in range(tm // CHUNK):
            rws = slice(ch * CHUNK, (ch + 1) * CHUNK)
            f = jnp.dot(w, vn_ref[rws, cols].astype(BF16), preferred_element_type=F32) + bsp_ref[grp]
            s_ref[rws, cols] = gu_ref[rws, cols] * f
    mix_in = sga_ref[...] * o_ref[...] + sgm_ref[...] * s_ref[...]
    mix = jnp.dot(mix_in.astype(BF16), wo_ref[...], preferred_element_type=F32)
    h2 = _layer_norm(ALPHA * h_ref[...] + mix, g2_ref[...], b2_ref[...])
    y = _swiglu(h2.astype(BF16), wg_ref, wu_ref, wd_ref)
    y_ref[...] = _layer_norm(ALPHA * h2 + 0.5 * y, g3_ref[...], b3_ref[...])


def _mix_ffn(h, o, sga, sgm, gu, vn, wsp, bsp, wo, g2, b2, wg, wu, wd, g3, b3, *, tm, period):
    rows = h.shape[0]
    row_spec = pl.BlockSpec((tm, D_MODEL), lambda i: (i, 0))
    consts = (wsp, bsp, wo, g2, b2, wg, wu, wd, g3, b3)
    return pl.pallas_call(
        functools.partial(_mix_ffn_kernel, period=period),
        grid=(rows // tm,),
        in_specs=[row_spec] * 6 + [_const_spec(a.shape) for a in consts],
        out_specs=row_spec,
        out_shape=jax.ShapeDtypeStruct((rows, D_MODEL), F32),
        scratch_shapes=[pltpu.VMEM((tm, D_MODEL), F32)],
        compiler_params=_params(1),
        name="mix_ffn",
    )(h, o, sga, sgm, gu, vn, *consts)


def _spatial_params(w_spatial, b_spatial, period):
    if period == CHUNK:
        w, b = w_spatial, b_spatial
    else:
        sel = np.equal.outer(np.arange(CHUNK) % period, np.arange(period)).astype(np.float32)
        hi = lax.Precision.HIGHEST
        w = jnp.einsum("ra,gab,cb->grc", sel, w_spatial[:, :period, :period], sel, precision=hi)
        b = jnp.einsum("ra,ga->gr", sel, b_spatial[:, :period], precision=hi)
    return w, b[:, :, None]


def kernel(x_prompt, x_sample, cache_k, cache_v, page_table, ln1_g, ln1_b, ffn1_wg, ffn1_wu, ffn1_wd, w_in, lam_q1, lam_k1, lam_q2, lam_k2, subln_g, gmlp_ln_g, gmlp_ln_b, w_spatial, b_spatial, w_out, ln2_g, ln2_b, ffn2_wg, ffn2_wu, ffn2_wd, ln3_g, ln3_b):
    batch, seq, _ = x_prompt.shape
    db, n_tok, _ = x_sample.shape
    rows_p, rows_s = batch * seq, db * n_tok
    assert rows_s == CHUNK and CHUNK % n_tok == 0 and seq % ATTN_TK == 0
    hp = x_prompt.reshape(rows_p, D_MODEL)
    hs = x_sample.reshape(rows_s, D_MODEL)
    row = lambda a: a.reshape(1, -1)
    tm = ATTN_TQ
    kp_l, vp_l, ks_l, vs_l, gv_l = [], [], [], [], []
    for l in range(DEPTH):
        lam_init = 0.8 - 0.6 * math.exp(-0.3 * l)
        wg1, wu1, wd1 = ffn1_wg[l].astype(BF16), ffn1_wu[l].astype(BF16), ffn1_wd[l].astype(BF16)
        wg2, wu2, wd2 = ffn2_wg[l].astype(BF16), ffn2_wu[l].astype(BF16), ffn2_wd[l].astype(BF16)
        w_in_b, w_out_b = w_in[l].astype(BF16), w_out[l].astype(BF16)
        lam_p = jnp.stack([lam_q1[l], lam_k1[l], lam_q2[l], lam_k2[l]])
        sub_g = row(subln_g[l])
        ffn1 = (wg1, wu1, wd1, row(ln1_g[l]), row(ln1_b[l]))
        tail = (w_out_b, row(ln2_g[l]), row(ln2_b[l]), wg2, wu2, wd2, row(ln3_g[l]), row(ln3_b[l]))
        gln = (row(gmlp_ln_g[l]), row(gmlp_ln_b[l]))

        h1 = _ffn_ln(hp, *ffn1, tm=tm)
        k_p, v_p, qt, kb, vt, gu, vn, sga, sgm = _proj(h1, w_in_b, *gln, tm=tm, batch=batch)
        o = _attn_prompt(qt, kb, vt, lam_p, sub_g, lam_init=lam_init)
        wsp, bsp = _spatial_params(w_spatial[l], b_spatial[l], CHUNK)
        hp = _mix_ffn(h1, o, sga, sgm, gu, vn, wsp, bsp, *tail, tm=tm, period=CHUNK)

        h1 = _ffn_ln(hs, *ffn1, tm=rows_s)
        q_s, k_s, v_s, gu, vn_s, sga, sgm = _proj(h1, w_in_b, *gln, tm=rows_s)
        tok = lambda a: a.reshape(db, n_tok * N_HEADS, V_DIM)
        o = _attn_decode(tok(q_s), tok(k_s), tok(v_s),
                         cache_k[l].reshape(-1, PAGE_SIZE * N_HEADS, V_DIM),
                         cache_v[l].reshape(-1, PAGE_SIZE * N_HEADS, V_DIM),
                         page_table, lam_p, sub_g, lam_init=lam_init)
        wsp, bsp = _spatial_params(w_spatial[l], b_spatial[l], n_tok)
        hs = _mix_ffn(h1, o.reshape(rows_s, D_MODEL), sga, sgm, gu, vn_s, wsp, bsp, *tail, tm=rows_s, period=n_tok)

        kp_l.append(k_p.reshape(batch, seq, N_HEADS, V_DIM))
        vp_l.append(v_p.reshape(batch, seq, N_HEADS, V_DIM))
        ks_l.append(k_s.reshape(db, n_tok, N_HEADS, V_DIM))
        vs_l.append(v_s.reshape(db, n_tok, N_HEADS, V_DIM))
        gv_l.append(vn_s.reshape(db, n_tok, D_MODEL))
    return (hp.reshape(batch, seq, D_MODEL), hs.reshape(db, n_tok, D_MODEL),
            jnp.stack(kp_l), jnp.stack(vp_l), jnp.stack(ks_l), jnp.stack(vs_l), jnp.stack(gv_l))
```

```python
import functools
import math

import jax
import jax.numpy as jnp
import numpy as np
from jax import lax
from jax.experimental import pallas as pl
from jax.experimental.pallas import tpu as pltpu

D_MODEL = 1024
DEPTH = 1
PAGE_SIZE = 128
N_HEADS = 8
HEAD_DIM = 64
V_DIM = 2 * HEAD_DIM
CHUNK = 128
GROUP_WIDTH = 128
N_GROUPS = D_MODEL // GROUP_WIDTH
D_FF = 2816
ALPHA = (2 * DEPTH) ** 0.25
LN_EPS = 1e-5
NEG_INF = -1e30
QK_SCALE = HEAD_DIM ** -0.5
LOG2E = math.log2(math.e)
Q_PRESCALE = QK_SCALE * LOG2E

VMEM_LIMIT_BYTES = 56 * 1024 * 1024
ATTN_TQ = 256
ATTN_TK = 1024
HEADS_PER_STEP = 2
PAGES_PER_STEP = 8
PAGE_SLOTS = 3

F32 = jnp.float32
BF16 = jnp.bfloat16
BF16_EXACT_INT = 256
BF16_ROWS_PER_VREG = 16


def _bf16_terms(x, n):
    terms = []
    for _ in range(n):
        m, e = math.frexp(x)
        t = math.ldexp(round(m * BF16_EXACT_INT) / BF16_EXACT_INT, e)
        terms.append(t)
        x -= t
    return tuple(terms)


LOG2E_BF16_TERMS = _bf16_terms(LOG2E, 4)


def _slopes():
    return [2.0 ** (-8.0 * (h + 1) / N_HEADS) for h in range(N_HEADS)]


def _layer_norm(x, g, b):
    mu = jnp.mean(x, axis=-1, keepdims=True)
    xc = x - mu
    var = jnp.mean(xc * xc, axis=-1, keepdims=True)
    return xc * lax.rsqrt(var + LN_EPS) * g + b


def _gelu(x):
    return 0.5 * x * (1.0 + lax.erf(x * math.sqrt(0.5)))


def _swiglu(xb, wg_ref, wu_ref, wd_ref):
    gate = jnp.dot(xb, wg_ref[...], preferred_element_type=F32)
    up = jnp.dot(xb, wu_ref[...], preferred_element_type=F32)
    act = (gate * jax.nn.sigmoid(gate)) * up
    return jnp.dot(act.astype(BF16), wd_ref[...], preferred_element_type=F32)


def _diff_lambda(lam_ref, lam_init):
    lp = lam_ref[...]
    a = jnp.sum(lp[0:1] * lp[1:2], axis=-1, keepdims=True)
    b = jnp.sum(lp[2:3] * lp[3:4], axis=-1, keepdims=True)
    return jnp.exp(a) - jnp.exp(b) + lam_init


def _const_spec(shape):
    nd = len(shape)
    return pl.BlockSpec(shape, lambda *_: (0,) * nd, pipeline_mode=pl.Buffered(1))


def _params(n_axes):
    return pltpu.CompilerParams(
        dimension_semantics=("arbitrary",) * n_axes, vmem_limit_bytes=VMEM_LIMIT_BYTES)


def _ffn_ln_kernel(x_ref, wg_ref, wu_ref, wd_ref, g_ref, b_ref, o_ref):
    x = x_ref[...]
    y = _swiglu(x.astype(BF16), wg_ref, wu_ref, wd_ref)
    o_ref[...] = _layer_norm(ALPHA * x + 0.5 * y, g_ref[...], b_ref[...])


def _ffn_ln(x, wg, wu, wd, g, b, *, tm):
    rows = x.shape[0]
    row_spec = pl.BlockSpec((tm, D_MODEL), lambda i: (i, 0))
    return pl.pallas_call(
        _ffn_ln_kernel,
        grid=(rows // tm,),
        in_specs=[row_spec, _const_spec(wg.shape), _const_spec(wu.shape), _const_spec(wd.shape),
                  _const_spec(g.shape), _const_spec(b.shape)],
        out_specs=row_spec,
        out_shape=jax.ShapeDtypeStruct((rows, D_MODEL), F32),
        compiler_params=_params(1),
        name="ffn_ln",
    )(x, wg, wu, wd, g, b)


def _proj_kernel(h_ref, w_ref, lg_ref, lb_ref, *out_refs, attn_layouts):
    hb = h_ref[...].astype(BF16)

    def section(i):
        return jnp.dot(hb, w_ref[:, i * D_MODEL:(i + 1) * D_MODEL], preferred_element_type=F32)

    q = section(0) * Q_PRESCALE
    k = section(1)
    v = section(2)
    if attn_layouts:
        k_ref, v_ref, qt_ref, kb_ref, vt_ref, gu_ref, vn_ref, sga_ref, sgm_ref = out_refs
        qt_ref[0, 0] = q.T.astype(BF16)
        kb_ref[...] = k.astype(BF16)
        vt_ref[0, 0] = v.T.astype(BF16)
    else:
        q_ref, k_ref, v_ref, gu_ref, vn_ref, sga_ref, sgm_ref = out_refs
        q_ref[...] = q.reshape(q.shape[0], N_HEADS, V_DIM)
    k_ref[...] = k.reshape(k.shape[0], N_HEADS, V_DIM)
    v_ref[...] = v.reshape(v.shape[0], N_HEADS, V_DIM)
    gu_ref[...] = _gelu(section(3))
    vn_ref[...] = _layer_norm(_gelu(section(4)), lg_ref[...], lb_ref[...])
    sga_ref[...] = jax.nn.sigmoid(section(5))
    sgm_ref[...] = jax.nn.sigmoid(section(6))


def _proj(h, w_in, ln_g, ln_b, *, tm, batch=None):
    rows = h.shape[0]
    n_tiles = rows // tm
    row_spec = pl.BlockSpec((tm, D_MODEL), lambda i: (i, 0))
    row_f32 = jax.ShapeDtypeStruct((rows, D_MODEL), F32)
    head_spec = pl.BlockSpec((tm, N_HEADS, V_DIM), lambda i: (i, 0, 0))
    head_f32 = jax.ShapeDtypeStruct((rows, N_HEADS, V_DIM), F32)
    attn_layouts = batch is not None
    if attn_layouts:
        per_b = n_tiles // batch
        t_spec = pl.BlockSpec((1, 1, D_MODEL, tm), lambda i: (i // per_b, i % per_b, 0, 0))
        t_shape = jax.ShapeDtypeStruct((batch, per_b, D_MODEL, tm), BF16)
        out_specs = [head_spec, head_spec, t_spec, row_spec, t_spec] + [row_spec] * 4
        out_shape = [head_f32, head_f32, t_shape, jax.ShapeDtypeStruct((rows, D_MODEL), BF16), t_shape] + [row_f32] * 4
    else:
        out_specs = [head_spec] * 3 + [row_spec] * 4
        out_shape = [head_f32] * 3 + [row_f32] * 4
    return pl.pallas_call(
        functools.partial(_proj_kernel, attn_layouts=attn_layouts),
        grid=(n_tiles,),
        in_specs=[row_spec, _const_spec(w_in.shape), _const_spec(ln_g.shape), _const_spec(ln_b.shape)],
        out_specs=out_specs,
        out_shape=out_shape,
        compiler_params=_params(1),
        name="proj",
    )(h, w_in, ln_g, ln_b)


def _attn_prompt_kernel(qt_ref, k_ref, vt_ref, pos_ref, rel_ref, slope_ref, lam_ref, g_ref, o_ref,
                        w_ref, acc_ref, s_ref, *, lam_init):
    tq, tk, nh = ATTN_TQ, ATTN_TK, HEADS_PER_STEP
    sub = tk // tq
    qi = pl.program_id(2)
    n_full = qi // sub
    zeros = jnp.zeros((HEAD_DIM, tq), BF16)
    prow = lax.broadcasted_iota(jnp.int32, (V_DIM, 2 * tq), 0)
    pos_rows = jnp.zeros((V_DIM, 2 * tq), F32)
    for i, term in enumerate(LOG2E_BF16_TERMS * 2):
        pos_rows = jnp.where(prow == i, term, pos_rows)
    pos_rows = pos_rows.astype(BF16)
    for h in range(nh):
        qt = qt_ref[0, 0, h * V_DIM:(h + 1) * V_DIM, :]
        w_ref[h, 0:HEAD_DIM, 0:tq] = qt[0:HEAD_DIM]
        w_ref[h, HEAD_DIM:V_DIM, 0:tq] = zeros
        w_ref[h, 0:HEAD_DIM, tq:] = zeros
        w_ref[h, HEAD_DIM:V_DIM, tq:] = qt[HEAD_DIM:]
        w_ref[h, V_DIM:, :] = pos_rows

    def head_cols(h):
        return slice(h * V_DIM, (h + 1) * V_DIM)

    def with_ones(vt):
        return jnp.concatenate([vt, jnp.ones((BF16_ROWS_PER_VREG, vt.shape[1]), BF16)], axis=0)

    def absorb(j, src_ref, cmax, ml):
        out = []
        for h in range(nh):
            m, l = ml[2 * h], ml[2 * h + 1]
            d = slope_ref[h] * jnp.full((1, 2 * tq), j * tk - qi * tq, jnp.int32).astype(F32)
            m_new = jnp.maximum(m, cmax[h] + d)
            a = jnp.exp2(m - m_new)
            p = jnp.exp2(src_ref[h] - (m_new - d))
            vt = jnp.concatenate([vt_ref[0, j * sub + u, head_cols(h), :] for u in range(sub)], axis=1)
            pv = jnp.dot(with_ones(vt), p.astype(BF16), preferred_element_type=F32)
            acc_ref[h] = a * acc_ref[h] + pv[:V_DIM]
            out += [m_new, a * l + pv[V_DIM:V_DIM + 1]]
        return tuple(out)

    def diagonal(r, with_scores0):
        def f():
            nk = (r + 1) * tq
            cm = [None if with_scores0 else jnp.full((1, 2 * tq), NEG_INF, F32)] * nh

            def scores0(u):
                if not with_scores0:
                    return
                rows = slice(u * tq, (u + 1) * tq)
                for h in range(nh):
                    lhs = jnp.concatenate([k_ref[0, 0, rows, head_cols(h)], pos_ref[h, rows]], axis=1)
                    s0 = jnp.dot(lhs, w_ref[h], preferred_element_type=F32)
                    s_ref[h, rows] = s0
                    bm = jnp.max(s0, axis=0, keepdims=True)
                    cm[h] = bm if cm[h] is None else jnp.maximum(cm[h], bm)

            s_d = []
            for h in range(nh):
                lhs = jnp.concatenate([k_ref[0, n_full, 0:nk, head_cols(h)], pos_ref[h, 0:nk]], axis=1)
                s_d.append(jnp.dot(lhs, w_ref[h], preferred_element_type=F32))
            scores0(0)
            ms, ds = [], []
            for h in range(nh):
                tail = jnp.where(rel_ref[...] <= 0.0, s_d[h][r * tq:], NEG_INF)
                s_d[h] = tail if r == 0 else jnp.concatenate([s_d[h][:r * tq], tail], axis=0)
                d = slope_ref[h] * float(-r * tq)
                ms.append(jnp.max(s_d[h], axis=0, keepdims=True) + d)
                ds.append(d)
            scores0(1)
            ps = [jnp.exp2(s_d[h] - (ms[h] - ds[h])) for h in range(nh)]
            scores0(2)
            out = []
            for h in range(nh):
                vt = jnp.concatenate([vt_ref[0, n_full * sub + u, head_cols(h), :] for u in range(r + 1)], axis=1)
                pv = jnp.dot(with_ones(vt), ps[h].astype(BF16), preferred_element_type=F32)
                acc_ref[h] = pv[:V_DIM]
                out += [ms[h], pv[V_DIM:V_DIM + 1]]
            scores0(3)
            return tuple(out) + tuple(cm)
        return f

    branches = [diagonal(r, False) for r in range(sub)] + [diagonal(r, True) for r in range(sub)]
    state = lax.switch(qi % sub + sub * (n_full > 0).astype(jnp.int32), branches)

    def step(j, state):
        ml, cmax = state[:2 * nh], state[2 * nh:]
        out, cmax_next = [], []
        for h in range(nh):
            m, l = ml[2 * h], ml[2 * h + 1]
            d = slope_ref[h] * jnp.full((1, 2 * tq), j * tk - qi * tq, jnp.int32).astype(F32)
            m_new = jnp.maximum(m, cmax[h] + d)
            a = jnp.exp2(m - m_new)
            shift = m_new - d
            cm, lsum, pv = None, None, None
            for u in range(sub):
                rows = slice(u * tq, (u + 1) * tq)
                lhs = jnp.concatenate([k_ref[0, j + 1, rows, head_cols(h)], pos_ref[h, rows]], axis=1)
                s = jnp.dot(lhs, w_ref[h], preferred_element_type=F32)
                p = jnp.exp2(s_ref[h, rows] - shift)
                s_ref[h, rows] = s
                bm = jnp.max(s, axis=0, keepdims=True)
                cm = bm if cm is None else jnp.maximum(cm, bm)
                part = jnp.dot(with_ones(vt_ref[0, j * sub + u, head_cols(h), :]), p.astype(BF16),
                               preferred_element_type=F32)
                pv = part if pv is None else pv + part
            lsum = pv[V_DIM:V_DIM + 1]
            pv = pv[:V_DIM]
            acc_ref[h] = a * acc_ref[h] + pv
            out += [m_new, a * l + lsum]
            cmax_next.append(cm)
        return tuple(out) + tuple(cmax_next)

    state = lax.fori_loop(0, n_full - 1, step, state)
    ml, cmax = state[:2 * nh], state[2 * nh:]
    carry = lax.cond(n_full > 0, lambda: absorb(n_full - 1, s_ref, cmax, ml), lambda: ml)

    lam = _diff_lambda(lam_ref, lam_init)
    for h in range(nh):
        on = acc_ref[h] / carry[2 * h + 1]
        ot = on[:, 0:tq] - lam * on[:, tq:]
        ot = ot * lax.rsqrt(jnp.mean(ot * ot, axis=0, keepdims=True) + LN_EPS)
        o_ref[:, h * V_DIM:(h + 1) * V_DIM] = ot.T * g_ref[...] * (1.0 - lam_init)


def _attn_prompt(qt, kb, vt, lam_p, subln_g, *, lam_init):
    batch, nblk, _, tq = qt.shape
    tk, nh = ATTN_TK, HEADS_PER_STEP
    assert tq == ATTN_TQ and (nblk * tq) % tk == 0
    nchunk = nblk * tq // tk
    kb = kb.reshape(batch, nchunk, tk, D_MODEL)
    slopes = np.asarray(_slopes(), np.float32)
    c = np.arange(tk)
    pos = np.zeros((N_HEADS, tk, V_DIM), np.float32)
    n_terms = len(LOG2E_BF16_TERMS)
    for i in range(n_terms):
        pos[:, :, i] = slopes[:, None] * ((c // BF16_EXACT_INT) * BF16_EXACT_INT)[None]
        pos[:, :, n_terms + i] = slopes[:, None] * (c % BF16_EXACT_INT)[None]
    pos = jnp.asarray(pos, BF16)
    rel = (c[:tq, None] - np.tile(np.arange(tq), 2)[None, :]).astype(np.float32)
    slope_row = np.ascontiguousarray(np.broadcast_to((slopes * LOG2E)[:, None, None], (N_HEADS, 1, 2 * tq)))
    return pl.pallas_call(
        functools.partial(_attn_prompt_kernel, lam_init=lam_init),
        grid=(batch, N_HEADS // nh, nblk),
        in_specs=[
            pl.BlockSpec((1, 1, nh * V_DIM, tq), lambda b, h, i: (b, i, h, 0)),
            pl.BlockSpec((1, nchunk, tk, nh * V_DIM), lambda b, h, i: (b, 0, 0, h)),
            pl.BlockSpec((1, nblk, nh * V_DIM, tq), lambda b, h, i: (b, 0, h, 0)),
            pl.BlockSpec((nh, tk, V_DIM), lambda b, h, i: (h, 0, 0)),
            pl.BlockSpec(rel.shape, lambda b, h, i: (0, 0), pipeline_mode=pl.Buffered(1)),
            pl.BlockSpec((nh, 1, 2 * tq), lambda b, h, i: (h, 0, 0)),
            pl.BlockSpec(lam_p.shape, lambda b, h, i: (0, 0)),
            pl.BlockSpec(subln_g.shape, lambda b, h, i: (0, 0)),
        ],
        out_specs=pl.BlockSpec((tq, nh * V_DIM), lambda b, h, i: (b * nblk + i, h)),
        out_shape=jax.ShapeDtypeStruct((batch * nblk * tq, D_MODEL), F32),
        scratch_shapes=[pltpu.VMEM((nh, 2 * V_DIM, 2 * tq), BF16), pltpu.VMEM((nh, V_DIM, 2 * tq), F32),
                        pltpu.VMEM((nh, tk, 2 * tq), F32)],
        compiler_params=_params(3),
        name="attn_prompt",
    )(qt, kb, vt, pos, rel, slope_row, lam_p, subln_g)


def _attn_decode_kernel(pt_ref, q_ref, kn_ref, vn_ref, ck_hbm, cv_hbm, bias_ref, biasn_ref, slope_ref, lam_ref,
                        g_ref, o_ref, kbuf, vbuf, sem, qm_ref, m_ref, l_ref, acc_ref, *, lam_init, past_len, n_tok):
    g_pages = PAGES_PER_STEP
    step = pl.program_id(1)
    n_steps = pl.num_programs(1)
    rows = n_tok * N_HEADS
    t = pl.program_id(0) * n_steps + step
    t_end = pl.num_programs(0) * n_steps

    def page_copies(tt, slot):
        b, s = tt // n_steps, tt % n_steps
        copies = []
        for g in range(g_pages):
            page = pt_ref[b, s * g_pages + g]
            copies.append(pltpu.make_async_copy(ck_hbm.at[page], kbuf.at[slot, g], sem.at[slot]))
            copies.append(pltpu.make_async_copy(cv_hbm.at[page], vbuf.at[slot, g], sem.at[slot]))
        return copies

    @pl.when(t == 0)
    def _():
        for ahead in range(PAGE_SLOTS - 1):
            for c in page_copies(ahead, ahead):
                c.start()

    nxt = t + (PAGE_SLOTS - 1)

    @pl.when(nxt < t_end)
    def _():
        for c in page_copies(nxt, nxt % PAGE_SLOTS):
            c.start()

    slot = t % PAGE_SLOTS
    for c in page_copies(t, slot):
        c.wait()

    @pl.when(step == 0)
    def _():
        q = q_ref[0]
        lane = lax.broadcasted_iota(jnp.int32, q.shape, 1)
        qm_ref[0:rows, :] = jnp.where(lane < HEAD_DIM, q, 0.0).astype(BF16)
        qm_ref[rows:, :] = jnp.where(lane >= HEAD_DIM, q, 0.0).astype(BF16)
        m_ref[...] = jnp.full_like(m_ref, NEG_INF)
        l_ref[...] = jnp.zeros_like(l_ref)
        acc_ref[...] = jnp.zeros_like(acc_ref)

    def update(pages):
        qm = qm_ref[...]
        s_list, mx = [], None
        for kp, _, bias, d in pages:
            s = lax.dot_general(qm, kp, (((1,), (1,)), ((), ())), preferred_element_type=F32) + bias
            s_list.append(s)
            blk = jnp.max(s, axis=1, keepdims=True) + d
            mx = blk if mx is None else jnp.maximum(mx, blk)
        m = m_ref[...]
        m_new = jnp.maximum(m, mx)
        a = jnp.exp2(m - m_new)
        l = a * l_ref[...]
        acc = a * acc_ref[...]
        for s, (_, vp, _, d) in zip(s_list, pages):
            p = jnp.exp2(s - (m_new - d))
            l = l + jnp.sum(p, axis=1, keepdims=True)
            acc = acc + jnp.dot(p.astype(BF16), vp, preferred_element_type=F32)
        l_ref[...] = l
        acc_ref[...] = acc
        m_ref[...] = m_new

    slope = slope_ref[...]
    pages = []
    for g in range(g_pages):
        page = step * g_pages + g
        d = slope * jnp.full((2 * rows, 1), page * PAGE_SIZE - past_len, jnp.int32).astype(F32)
        pages.append((kbuf[slot, g].astype(BF16), vbuf[slot, g].astype(BF16), bias_ref[...], d))
    update(pages)

    @pl.when(step == pl.num_programs(1) - 1)
    def _():
        update([(kn_ref[0].astype(BF16), vn_ref[0].astype(BF16), biasn_ref[...], jnp.zeros((2 * rows, 1), F32))])
        lam = _diff_lambda(lam_ref, lam_init)
        on = acc_ref[...] / l_ref[...]
        o = on[0:rows] - lam * on[rows:]
        o = o * lax.rsqrt(jnp.mean(o * o, axis=-1, keepdims=True) + LN_EPS)
        o_ref[0] = o * g_ref[...] * (1.0 - lam_init)


def _attn_decode(q, k_new, v_new, cache_k, cache_v, page_table, lam_p, subln_g, *, lam_init):
    db, rows, _ = q.shape
    n_tok = rows // N_HEADS
    n_pages = page_table.shape[1]
    past_len = n_pages * PAGE_SIZE
    g_pages = PAGES_PER_STEP
    assert n_pages % g_pages == 0 and db * (n_pages // g_pages) >= PAGE_SLOTS - 1
    slopes = np.asarray(_slopes(), np.float32)
    row_h = np.tile(np.arange(N_HEADS), 2 * n_tok)
    row_t = np.tile(np.repeat(np.arange(n_tok), N_HEADS), 2)
    row_slope = (slopes * LOG2E)[row_h]
    col_pos = np.repeat(np.arange(PAGE_SIZE), N_HEADS)
    col_h = np.tile(np.arange(N_HEADS), PAGE_SIZE)
    bias = np.where(row_h[:, None] == col_h[None, :],
                    row_slope[:, None] * (col_pos[None, :] - row_t[:, None]), NEG_INF).astype(np.float32)
    ncol_t = np.repeat(np.arange(n_tok), N_HEADS)
    ncol_h = np.tile(np.arange(N_HEADS), n_tok)
    dist = row_t[:, None] - ncol_t[None, :]
    bias_new = np.where((row_h[:, None] == ncol_h[None, :]) & (dist >= 0),
                        -row_slope[:, None] * dist, NEG_INF).astype(np.float32)
    slope_col = np.ascontiguousarray(row_slope[:, None])

    tok_spec = pl.BlockSpec((1, rows, V_DIM), lambda b, s, pt: (b, 0, 0))

    def const2(shape):
        return pl.BlockSpec(shape, lambda b, s, pt: (0, 0))

    hbm = pl.BlockSpec(memory_space=pl.ANY)
    page_block = (PAGE_SLOTS, g_pages, PAGE_SIZE * N_HEADS, V_DIM)
    grid_spec = pltpu.PrefetchScalarGridSpec(
        num_scalar_prefetch=1,
        grid=(db, n_pages // g_pages),
        in_specs=[tok_spec, tok_spec, tok_spec, hbm, hbm,
                  const2(bias.shape), const2(bias_new.shape), const2(slope_col.shape),
                  const2(lam_p.shape), const2(subln_g.shape)],
        out_specs=tok_spec,
        scratch_shapes=[pltpu.VMEM(page_block, F32), pltpu.VMEM(page_block, F32),
                        pltpu.SemaphoreType.DMA((PAGE_SLOTS,)),
                        pltpu.VMEM((2 * rows, V_DIM), BF16), pltpu.VMEM((2 * rows, 1), F32),
                        pltpu.VMEM((2 * rows, 1), F32), pltpu.VMEM((2 * rows, V_DIM), F32)],
    )
    return pl.pallas_call(
        functools.partial(_attn_decode_kernel, lam_init=lam_init, past_len=past_len, n_tok=n_tok),
        grid_spec=grid_spec,
        out_shape=jax.ShapeDtypeStruct((db, rows, V_DIM), F32),
        compiler_params=_params(2),
        name="attn_decode",
    )(page_table, q, k_new, v_new, cache_k, cache_v, bias, bias_new, slope_col, lam_p, subln_g)


def _mix_ffn_kernel(h_ref, o_ref, sga_ref, sgm_ref, gu_ref, vn_ref, wsp_ref, bsp_ref, wo_ref,
                    g2_ref, b2_ref, wg_ref, wu_ref, wd_ref, g3_ref, b3_ref, y_ref, s_ref, *, period):
    tm = h_ref.shape[0]
    r = lax.broadcasted_iota(jnp.int32, (CHUNK, CHUNK), 0)
    c = lax.broadcasted_iota(jnp.int32, (CHUNK, CHUNK), 1)
    keep = (r // period == c // period) & (c <= r)
    for grp in range(N_GROUPS):
        w = jnp.where(keep, wsp_ref[grp], 0.0).astype(BF16)
        cols = slice(grp * GROUP_WIDTH, (grp + 1) * GROUP_WIDTH)
        for ch in range(tm // CHUNK):
            rws = slice(ch * CHUNK, (ch + 1) * CHUNK)
            f = jnp.dot(w, vn_ref[rws, cols].astype(BF16), preferred_element_type=F32) + bsp_ref[grp]
            s_ref[rws, cols] = gu_ref[rws, cols] * f
    mix_in = sga_ref[...] * o_ref[...] + sgm_ref[...] * s_ref[...]
    mix = jnp.dot(mix_in.astype(BF16), wo_ref[...], preferred_element_type=F32)
    h2 = _layer_norm(ALPHA * h_ref[...] + mix, g2_ref[...], b2_ref[...])
    y = _swiglu(h2.astype(BF16), wg_ref, wu_ref, wd_ref)
    y_ref[...] = _layer_norm(ALPHA * h2 + 0.5 * y, g3_ref[...], b3_ref[...])


def _mix_ffn(h, o, sga, sgm, gu, vn, wsp, bsp, wo, g2, b2, wg, wu, wd, g3, b3, *, tm, period):
    rows = h.shape[0]
    row_spec = pl.BlockSpec((tm, D_MODEL), lambda i: (i, 0))
    consts = (wsp, bsp, wo, g2, b2, wg, wu, wd, g3, b3)
    return pl.pallas_call(
        functools.partial(_mix_ffn_kernel, period=period),
        grid=(rows // tm,),
        in_specs=[row_spec] * 6 + [_const_spec(a.shape) for a in consts],
        out_specs=row_spec,
        out_shape=jax.ShapeDtypeStruct((rows, D_MODEL), F32),
        scratch_shapes=[pltpu.VMEM((tm, D_MODEL), F32)],
        compiler_params=_params(1),
        name="mix_ffn",
    )(h, o, sga, sgm, gu, vn, *consts)


def _spatial_params(w_spatial, b_spatial, period):
    if period == CHUNK:
        w, b = w_spatial, b_spatial
    else:
        sel = np.equal.outer(np.arange(CHUNK) % period, np.arange(period)).astype(np.float32)
        hi = lax.Precision.HIGHEST
        w = jnp.einsum("ra,gab,cb->grc", sel, w_spatial[:, :period, :period], sel, precision=hi)
        b = jnp.einsum("ra,ga->gr", sel, b_spatial[:, :period], precision=hi)
    return w, b[:, :, None]


def kernel(x_prompt, x_sample, cache_k, cache_v, page_table, ln1_g, ln1_b, ffn1_wg, ffn1_wu, ffn1_wd, w_in, lam_q1, lam_k1, lam_q2, lam_k2, subln_g, gmlp_ln_g, gmlp_ln_b, w_spatial, b_spatial, w_out, ln2_g, ln2_b, ffn2_wg, ffn2_wu, ffn2_wd, ln3_g, ln3_b):
    batch, seq, _ = x_prompt.shape
    db, n_tok, _ = x_sample.shape
    rows_p, rows_s = batch * seq, db * n_tok
    assert rows_s == CHUNK and CHUNK % n_tok == 0 and seq % ATTN_TK == 0
    hp = x_prompt.reshape(rows_p, D_MODEL)
    hs = x_sample.reshape(rows_s, D_MODEL)
    row = lambda a: a.reshape(1, -1)
    tm = ATTN_TQ
    kp_l, vp_l, ks_l, vs_l, gv_l = [], [], [], [], []
    for l in range(DEPTH):
        lam_init = 0.8 - 0.6 * math.exp(-0.3 * l)
        wg1, wu1, wd1 = ffn1_wg[l].astype(BF16), ffn1_wu[l].astype(BF16), ffn1_wd[l].astype(BF16)
        wg2, wu2, wd2 = ffn2_wg[l].astype(BF16), ffn2_wu[l].astype(BF16), ffn2_wd[l].astype(BF16)
        w_in_b, w_out_b = w_in[l].astype(BF16), w_out[l].astype(BF16)
        lam_p = jnp.stack([lam_q1[l], lam_k1[l], lam_q2[l], lam_k2[l]])
        sub_g = row(subln_g[l])
        ffn1 = (wg1, wu1, wd1, row(ln1_g[l]), row(ln1_b[l]))
        tail = (w_out_b, row(ln2_g[l]), row(ln2_b[l]), wg2, wu2, wd2, row(ln3_g[l]), row(ln3_b[l]))
        gln = (row(gmlp_ln_g[l]), row(gmlp_ln_b[l]))

        h1 = _ffn_ln(hp, *ffn1, tm=tm)
        k_p, v_p, qt, kb, vt, gu, vn, sga, sgm = _proj(h1, w_in_b, *gln, tm=tm, batch=batch)
        o = _attn_prompt(qt, kb, vt, lam_p, sub_g, lam_init=lam_init)
        wsp, bsp = _spatial_params(w_spatial[l], b_spatial[l], CHUNK)
        hp = _mix_ffn(h1, o, sga, sgm, gu, vn, wsp, bsp, *tail, tm=tm, period=CHUNK)

        h1 = _ffn_ln(hs, *ffn1, tm=rows_s)
        q_s, k_s, v_s, gu, vn_s, sga, sgm = _proj(h1, w_in_b, *gln, tm=rows_s)
        tok = lambda a: a.reshape(db, n_tok * N_HEADS, V_DIM)
        o = _attn_decode(tok(q_s), tok(k_s), tok(v_s),
                         cache_k[l].reshape(-1, PAGE_SIZE * N_HEADS, V_DIM),
                         cache_v[l].reshape(-1, PAGE_SIZE * N_HEADS, V_DIM),
                         page_table, lam_p, sub_g, lam_init=lam_init)
        wsp, bsp = _spatial_params(w_spatial[l], b_spatial[l], n_tok)
        hs = _mix_ffn(h1, o.reshape(rows_s, D_MODEL), sga, sgm, gu, vn_s, wsp, bsp, *tail, tm=rows_s, period=n_tok)

        kp_l.append(k_p.reshape(batch, seq, N_HEADS, V_DIM))
        vp_l.append(v_p.reshape(batch, seq, N_HEADS, V_DIM))
        ks_l.append(k_s.reshape(db, n_tok, N_HEADS, V_DIM))
        vs_l.append(v_s.reshape(db, n_tok, N_HEADS, V_DIM))
        gv_l.append(vn_s.reshape(db, n_tok, D_MODEL))
    return (hp.reshape(batch, seq, D_MODEL), hs.reshape(db, n_tok, D_MODEL),
            jnp.stack(kp_l), jnp.stack(vp_l), jnp.stack(ks_l), jnp.stack(vs_l), jnp.stack(gv_l))
```

```python
import functools
import math

import jax
import jax.numpy as jnp
import numpy as np
from jax import lax
from jax.experimental import pallas as pl
from jax.experimental.pallas import tpu as pltpu

D_MODEL = 1024
DEPTH = 1
PAGE_SIZE = 128
N_HEADS = 8
HEAD_DIM = 64
V_DIM = 2 * HEAD_DIM
CHUNK = 128
GROUP_WIDTH = 128
N_GROUPS = D_MODEL // GROUP_WIDTH
D_FF = 2816
ALPHA = (2 * DEPTH) ** 0.25
LN_EPS = 1e-5
NEG_INF = -1e30
QK_SCALE = HEAD_DIM ** -0.5
LOG2E = math.log2(math.e)
Q_PRESCALE = QK_SCALE * LOG2E

VMEM_LIMIT_BYTES = 56 * 1024 * 1024
ATTN_TQ = 256
ATTN_TK = 1024
HEADS_PER_STEP = 4
PAGES_PER_STEP = 8
PAGE_SLOTS = 3

F32 = jnp.float32
BF16 = jnp.bfloat16
BF16_EXACT_INT = 256
BF16_ROWS_PER_VREG = 16


def _bf16_terms(x, n):
    terms = []
    for _ in range(n):
        m, e = math.frexp(x)
        t = math.ldexp(round(m * BF16_EXACT_INT) / BF16_EXACT_INT, e)
        terms.append(t)
        x -= t
    return tuple(terms)


LOG2E_BF16_TERMS = _bf16_terms(LOG2E, 4)


def _slopes():
    return [2.0 ** (-8.0 * (h + 1) / N_HEADS) for h in range(N_HEADS)]


def _layer_norm(x, g, b):
    mu = jnp.mean(x, axis=-1, keepdims=True)
    xc = x - mu
    var = jnp.mean(xc * xc, axis=-1, keepdims=True)
    return xc * lax.rsqrt(var + LN_EPS) * g + b


def _gelu(x):
    return 0.5 * x * (1.0 + lax.erf(x * math.sqrt(0.5)))


def _swiglu(xb, wg_ref, wu_ref, wd_ref):
    gate = jnp.dot(xb, wg_ref[...], preferred_element_type=F32)
    up = jnp.dot(xb, wu_ref[...], preferred_element_type=F32)
    act = (gate * jax.nn.sigmoid(gate)) * up
    return jnp.dot(act.astype(BF16), wd_ref[...], preferred_element_type=F32)


def _diff_lambda(lam_ref, lam_init):
    lp = lam_ref[...]
    a = jnp.sum(lp[0:1] * lp[1:2], axis=-1, keepdims=True)
    b = jnp.sum(lp[2:3] * lp[3:4], axis=-1, keepdims=True)
    return jnp.exp(a) - jnp.exp(b) + lam_init


def _const_spec(shape):
    nd = len(shape)
    return pl.BlockSpec(shape, lambda *_: (0,) * nd, pipeline_mode=pl.Buffered(1))


def _params(n_axes):
    return pltpu.CompilerParams(
        dimension_semantics=("arbitrary",) * n_axes, vmem_limit_bytes=VMEM_LIMIT_BYTES)


def _ffn_ln_kernel(x_ref, wg_ref, wu_ref, wd_ref, g_ref, b_ref, o_ref):
    x = x_ref[...]
    y = _swiglu(x.astype(BF16), wg_ref, wu_ref, wd_ref)
    o_ref[...] = _layer_norm(ALPHA * x + 0.5 * y, g_ref[...], b_ref[...])


def _ffn_ln(x, wg, wu, wd, g, b, *, tm):
    rows = x.shape[0]
    row_spec = pl.BlockSpec((tm, D_MODEL), lambda i: (i, 0))
    return pl.pallas_call(
        _ffn_ln_kernel,
        grid=(rows // tm,),
        in_specs=[row_spec, _const_spec(wg.shape), _const_spec(wu.shape), _const_spec(wd.shape),
                  _const_spec(g.shape), _const_spec(b.shape)],
        out_specs=row_spec,
        out_shape=jax.ShapeDtypeStruct((rows, D_MODEL), F32),
        compiler_params=_params(1),
        name="ffn_ln",
    )(x, wg, wu, wd, g, b)


def _proj_kernel(h_ref, w_ref, lg_ref, lb_ref, *out_refs, attn_layouts):
    hb = h_ref[...].astype(BF16)

    def section(i):
        return jnp.dot(hb, w_ref[:, i * D_MODEL:(i + 1) * D_MODEL], preferred_element_type=F32)

    q = section(0) * Q_PRESCALE
    k = section(1)
    v = section(2)
    if attn_layouts:
        k_ref, v_ref, qt_ref, kb_ref, vt_ref, gu_ref, vn_ref, sga_ref, sgm_ref = out_refs
        qt_ref[0, 0] = q.T.astype(BF16)
        kb_ref[...] = k.astype(BF16)
        vt_ref[0, 0] = v.T.astype(BF16)
    else:
        q_ref, k_ref, v_ref, gu_ref, vn_ref, sga_ref, sgm_ref = out_refs
        q_ref[...] = q.reshape(q.shape[0], N_HEADS, V_DIM)
    k_ref[...] = k.reshape(k.shape[0], N_HEADS, V_DIM)
    v_ref[...] = v.reshape(v.shape[0], N_HEADS, V_DIM)
    gu_ref[...] = _gelu(section(3))
    vn_ref[...] = _layer_norm(_gelu(section(4)), lg_ref[...], lb_ref[...])
    sga_ref[...] = jax.nn.sigmoid(section(5))
    sgm_ref[...] = jax.nn.sigmoid(section(6))


def _proj(h, w_in, ln_g, ln_b, *, tm, batch=None):
    rows = h.shape[0]
    n_tiles = rows // tm
    row_spec = pl.BlockSpec((tm, D_MODEL), lambda i: (i, 0))
    row_f32 = jax.ShapeDtypeStruct((rows, D_MODEL), F32)
    head_spec = pl.BlockSpec((tm, N_HEADS, V_DIM), lambda i: (i, 0, 0))
    head_f32 = jax.ShapeDtypeStruct((rows, N_HEADS, V_DIM), F32)
    attn_layouts = batch is not None
    if attn_layouts:
        per_b = n_tiles // batch
        t_spec = pl.BlockSpec((1, 1, D_MODEL, tm), lambda i: (i // per_b, i % per_b, 0, 0))
        t_shape = jax.ShapeDtypeStruct((batch, per_b, D_MODEL, tm), BF16)
        out_specs = [head_spec, head_spec, t_spec, row_spec, t_spec] + [row_spec] * 4
        out_shape = [head_f32, head_f32, t_shape, jax.ShapeDtypeStruct((rows, D_MODEL), BF16), t_shape] + [row_f32] * 4
    else:
        out_specs = [head_spec] * 3 + [row_spec] * 4
        out_shape = [head_f32] * 3 + [row_f32] * 4
    return pl.pallas_call(
        functools.partial(_proj_kernel, attn_layouts=attn_layouts),
        grid=(n_tiles,),
        in_specs=[row_spec, _const_spec(w_in.shape), _const_spec(ln_g.shape), _const_spec(ln_b.shape)],
        out_specs=out_specs,
        out_shape=out_shape,
        compiler_params=_params(1),
        name="proj",
    )(h, w_in, ln_g, ln_b)


def _attn_prompt_kernel(qt_ref, k_ref, vt_ref, pos_ref, rel_ref, slope_ref, lam_ref, g_ref, o_ref,
                        w_ref, acc_ref, s_ref, *, lam_init):
    tq, tk, nh = ATTN_TQ, ATTN_TK, HEADS_PER_STEP
    sub = tk // tq
    qi = pl.program_id(2)
    n_full = qi // sub
    zeros = jnp.zeros((HEAD_DIM, tq), BF16)
    prow = lax.broadcasted_iota(jnp.int32, (V_DIM, 2 * tq), 0)
    pos_rows = jnp.zeros((V_DIM, 2 * tq), F32)
    for i, term in enumerate(LOG2E_BF16_TERMS * 2):
        pos_rows = jnp.where(prow == i, term, pos_rows)
    pos_rows = pos_rows.astype(BF16)
    for h in range(nh):
        qt = qt_ref[0, 0, h * V_DIM:(h + 1) * V_DIM, :]
        w_ref[h, 0:HEAD_DIM, 0:tq] = qt[0:HEAD_DIM]
        w_ref[h, HEAD_DIM:V_DIM, 0:tq] = zeros
        w_ref[h, 0:HEAD_DIM, tq:] = zeros
        w_ref[h, HEAD_DIM:V_DIM, tq:] = qt[HEAD_DIM:]
        w_ref[h, V_DIM:, :] = pos_rows

    def head_cols(h):
        return slice(h * V_DIM, (h + 1) * V_DIM)

    def with_ones(vt):
        return jnp.concatenate([vt, jnp.ones((BF16_ROWS_PER_VREG, vt.shape[1]), BF16)], axis=0)

    def absorb(j, src_ref, cmax, ml):
        out = []
        for h in range(nh):
            m, l = ml[2 * h], ml[2 * h + 1]
            d = slope_ref[h] * jnp.full((1, 2 * tq), j * tk - qi * tq, jnp.int32).astype(F32)
            m_new = jnp.maximum(m, cmax[h] + d)
            a = jnp.exp2(m - m_new)
            p = jnp.exp2(src_ref[h] - (m_new - d))
            vt = jnp.concatenate([vt_ref[0, j * sub + u, head_cols(h), :] for u in range(sub)], axis=1)
            pv = jnp.dot(with_ones(vt), p.astype(BF16), preferred_element_type=F32)
            acc_ref[h] = a * acc_ref[h] + pv[:V_DIM]
            out += [m_new, a * l + pv[V_DIM:V_DIM + 1]]
        return tuple(out)

    def diagonal(r, with_scores0):
        def f():
            nk = (r + 1) * tq
            cm = [None if with_scores0 else jnp.full((1, 2 * tq), NEG_INF, F32)] * nh

            def scores0(u):
                if not with_scores0:
                    return
                rows = slice(u * tq, (u + 1) * tq)
                for h in range(nh):
                    lhs = jnp.concatenate([k_ref[0, 0, rows, head_cols(h)], pos_ref[h, rows]], axis=1)
                    s0 = jnp.dot(lhs, w_ref[h], preferred_element_type=F32)
                    s_ref[h, rows] = s0
                    bm = jnp.max(s0, axis=0, keepdims=True)
                    cm[h] = bm if cm[h] is None else jnp.maximum(cm[h], bm)

            s_d = []
            for h in range(nh):
                lhs = jnp.concatenate([k_ref[0, n_full, 0:nk, head_cols(h)], pos_ref[h, 0:nk]], axis=1)
                s_d.append(jnp.dot(lhs, w_ref[h], preferred_element_type=F32))
            scores0(0)
            ms, ds = [], []
            for h in range(nh):
                tail = jnp.where(rel_ref[...] <= 0.0, s_d[h][r * tq:], NEG_INF)
                s_d[h] = tail if r == 0 else jnp.concatenate([s_d[h][:r * tq], tail], axis=0)
                d = slope_ref[h] * float(-r * tq)
                ms.append(jnp.max(s_d[h], axis=0, keepdims=True) + d)
                ds.append(d)
            scores0(1)
            ps = [jnp.exp2(s_d[h] - (ms[h] - ds[h])) for h in range(nh)]
            scores0(2)
            out = []
            for h in range(nh):
                vt = jnp.concatenate([vt_ref[0, n_full * sub + u, head_cols(h), :] for u in range(r + 1)], axis=1)
                pv = jnp.dot(with_ones(vt), ps[h].astype(BF16), preferred_element_type=F32)
                acc_ref[h] = pv[:V_DIM]
                out += [ms[h], pv[V_DIM:V_DIM + 1]]
            scores0(3)
            return tuple(out) + tuple(cm)
        return f

    branches = [diagonal(r, False) for r in range(sub)] + [diagonal(r, True) for r in range(sub)]
    state = lax.switch(qi % sub + sub * (n_full > 0).astype(jnp.int32), branches)

    def step(j, state):
        ml, cmax = state[:2 * nh], state[2 * nh:]
        out, cmax_next = [], []
        for h in range(nh):
            m, l = ml[2 * h], ml[2 * h + 1]
            d = slope_ref[h] * jnp.full((1, 2 * tq), j * tk - qi * tq, jnp.int32).astype(F32)
            m_new = jnp.maximum(m, cmax[h] + d)
            a = jnp.exp2(m - m_new)
            shift = m_new - d
            cm, lsum, pv = None, None, None
            for u in range(sub):
                rows = slice(u * tq, (u + 1) * tq)
                lhs = jnp.concatenate([k_ref[0, j + 1, rows, head_cols(h)], pos_ref[h, rows]], axis=1)
                s = jnp.dot(lhs, w_ref[h], preferred_element_type=F32)
                p = jnp.exp2(s_ref[h, rows] - shift)
                s_ref[h, rows] = s
                bm = jnp.max(s, axis=0, keepdims=True)
                cm = bm if cm is None else jnp.maximum(cm, bm)
                part = jnp.dot(with_ones(vt_ref[0, j * sub + u, head_cols(h), :]), p.astype(BF16),
                               preferred_element_type=F32)
                pv = part if pv is None else pv + part
            lsum = pv[V_DIM:V_DIM + 1]
            pv = pv[:V_DIM]
            acc_ref[h] = a * acc_ref[h] + pv
            out += [m_new, a * l + lsum]
            cmax_next.append(cm)
        return tuple(out) + tuple(cmax_next)

    state = lax.fori_loop(0, n_full - 1, step, state)
    ml, cmax = state[:2 * nh], state[2 * nh:]
    carry = lax.cond(n_full > 0, lambda: absorb(n_full - 1, s_ref, cmax, ml), lambda: ml)

    lam = _diff_lambda(lam_ref, lam_init)
    for h in range(nh):
        on = acc_ref[h] / carry[2 * h + 1]
        ot = on[:, 0:tq] - lam * on[:, tq:]
        ot = ot * lax.rsqrt(jnp.mean(ot * ot, axis=0, keepdims=True) + LN_EPS)
        o_ref[:, h * V_DIM:(h + 1) * V_DIM] = ot.T * g_ref[...] * (1.0 - lam_init)


def _attn_prompt(qt, kb, vt, lam_p, subln_g, *, lam_init):
    batch, nblk, _, tq = qt.shape
    tk, nh = ATTN_TK, HEADS_PER_STEP
    assert tq == ATTN_TQ and (nblk * tq) % tk == 0
    nchunk = nblk * tq // tk
    kb = kb.reshape(batch, nchunk, tk, D_MODEL)
    slopes = np.asarray(_slopes(), np.float32)
    c = np.arange(tk)
    pos = np.zeros((N_HEADS, tk, V_DIM), np.float32)
    n_terms = len(LOG2E_BF16_TERMS)
    for i in range(n_terms):
        pos[:, :, i] = slopes[:, None] * ((c // BF16_EXACT_INT) * BF16_EXACT_INT)[None]
        pos[:, :, n_terms + i] = slopes[:, None] * (c % BF16_EXACT_INT)[None]
    pos = jnp.asarray(pos, BF16)
    rel = (c[:tq, None] - np.tile(np.arange(tq), 2)[None, :]).astype(np.float32)
    slope_row = np.ascontiguousarray(np.broadcast_to((slopes * LOG2E)[:, None, None], (N_HEADS, 1, 2 * tq)))
    return pl.pallas_call(
        functools.partial(_attn_prompt_kernel, lam_init=lam_init),
        grid=(batch, N_HEADS // nh, nblk),
        in_specs=[
            pl.BlockSpec((1, 1, nh * V_DIM, tq), lambda b, h, i: (b, i, h, 0)),
            pl.BlockSpec((1, nchunk, tk, nh * V_DIM), lambda b, h, i: (b, 0, 0, h)),
            pl.BlockSpec((1, nblk, nh * V_DIM, tq), lambda b, h, i: (b, 0, h, 0)),
            pl.BlockSpec((nh, tk, V_DIM), lambda b, h, i: (h, 0, 0)),
            pl.BlockSpec(rel.shape, lambda b, h, i: (0, 0), pipeline_mode=pl.Buffered(1)),
            pl.BlockSpec((nh, 1, 2 * tq), lambda b, h, i: (h, 0, 0)),
            pl.BlockSpec(lam_p.shape, lambda b, h, i: (0, 0)),
            pl.BlockSpec(subln_g.shape, lambda b, h, i: (0, 0)),
        ],
        out_specs=pl.BlockSpec((tq, nh * V_DIM), lambda b, h, i: (b * nblk + i, h)),
        out_shape=jax.ShapeDtypeStruct((batch * nblk * tq, D_MODEL), F32),
        scratch_shapes=[pltpu.VMEM((nh, 2 * V_DIM, 2 * tq), BF16), pltpu.VMEM((nh, V_DIM, 2 * tq), F32),
                        pltpu.VMEM((nh, tk, 2 * tq), F32)],
        compiler_params=_params(3),
        name="attn_prompt",
    )(qt, kb, vt, pos, rel, slope_row, lam_p, subln_g)


def _attn_decode_kernel(pt_ref, q_ref, kn_ref, vn_ref, ck_hbm, cv_hbm, bias_ref, biasn_ref, slope_ref, lam_ref,
                        g_ref, o_ref, kbuf, vbuf, sem, qm_ref, m_ref, l_ref, acc_ref, *, lam_init, past_len, n_tok):
    g_pages = PAGES_PER_STEP
    step = pl.program_id(1)
    n_steps = pl.num_programs(1)
    rows = n_tok * N_HEADS
    t = pl.program_id(0) * n_steps + step
    t_end = pl.num_programs(0) * n_steps

    def page_copies(tt, slot):
        b, s = tt // n_steps, tt % n_steps
        copies = []
        for g in range(g_pages):
            page = pt_ref[b, s * g_pages + g]
            copies.append(pltpu.make_async_copy(ck_hbm.at[page], kbuf.at[slot, g], sem.at[slot]))
            copies.append(pltpu.make_async_copy(cv_hbm.at[page], vbuf.at[slot, g], sem.at[slot]))
        return copies

    @pl.when(t == 0)
    def _():
        for ahead in range(PAGE_SLOTS - 1):
            for c in page_copies(ahead, ahead):
                c.start()

    nxt = t + (PAGE_SLOTS - 1)

    @pl.when(nxt < t_end)
    def _():
        for c in page_copies(nxt, nxt % PAGE_SLOTS):
            c.start()

    slot = t % PAGE_SLOTS
    for c in page_copies(t, slot):
        c.wait()

    @pl.when(step == 0)
    def _():
        q = q_ref[0]
        lane = lax.broadcasted_iota(jnp.int32, q.shape, 1)
        qm_ref[0:rows, :] = jnp.where(lane < HEAD_DIM, q, 0.0).astype(BF16)
        qm_ref[rows:, :] = jnp.where(lane >= HEAD_DIM, q, 0.0).astype(BF16)
        m_ref[...] = jnp.full_like(m_ref, NEG_INF)
        l_ref[...] = jnp.zeros_like(l_ref)
        acc_ref[...] = jnp.zeros_like(acc_ref)

    def update(pages):
        qm = qm_ref[...]
        s_list, mx = [], None
        for kp, _, bias, d in pages:
            s = lax.dot_general(qm, kp, (((1,), (1,)), ((), ())), preferred_element_type=F32) + bias
            s_list.append(s)
            blk = jnp.max(s, axis=1, keepdims=True) + d
            mx = blk if mx is None else jnp.maximum(mx, blk)
        m = m_ref[...]
        m_new = jnp.maximum(m, mx)
        a = jnp.exp2(m - m_new)
        l = a * l_ref[...]
        acc = a * acc_ref[...]
        for s, (_, vp, _, d) in zip(s_list, pages):
            p = jnp.exp2(s - (m_new - d))
            l = l + jnp.sum(p, axis=1, keepdims=True)
            acc = acc + jnp.dot(p.astype(BF16), vp, preferred_element_type=F32)
        l_ref[...] = l
        acc_ref[...] = acc
        m_ref[...] = m_new

    slope = slope_ref[...]
    pages = []
    for g in range(g_pages):
        page = step * g_pages + g
        d = slope * jnp.full((2 * rows, 1), page * PAGE_SIZE - past_len, jnp.int32).astype(F32)
        pages.append((kbuf[slot, g].astype(BF16), vbuf[slot, g].astype(BF16), bias_ref[...], d))
    update(pages)

    @pl.when(step == pl.num_programs(1) - 1)
    def _():
        update([(kn_ref[0].astype(BF16), vn_ref[0].astype(BF16), biasn_ref[...], jnp.zeros((2 * rows, 1), F32))])
        lam = _diff_lambda(lam_ref, lam_init)
        on = acc_ref[...] / l_ref[...]
        o = on[0:rows] - lam * on[rows:]
        o = o * lax.rsqrt(jnp.mean(o * o, axis=-1, keepdims=True) + LN_EPS)
        o_ref[0] = o * g_ref[...] * (1.0 - lam_init)


def _attn_decode(q, k_new, v_new, cache_k, cache_v, page_table, lam_p, subln_g, *, lam_init):
    db, rows, _ = q.shape
    n_tok = rows // N_HEADS
    n_pages = page_table.shape[1]
    past_len = n_pages * PAGE_SIZE
    g_pages = PAGES_PER_STEP
    assert n_pages % g_pages == 0 and db * (n_pages // g_pages) >= PAGE_SLOTS - 1
    slopes = np.asarray(_slopes(), np.float32)
    row_h = np.tile(np.arange(N_HEADS), 2 * n_tok)
    row_t = np.tile(np.repeat(np.arange(n_tok), N_HEADS), 2)
    row_slope = (slopes * LOG2E)[row_h]
    col_pos = np.repeat(np.arange(PAGE_SIZE), N_HEADS)
    col_h = np.tile(np.arange(N_HEADS), PAGE_SIZE)
    bias = np.where(row_h[:, None] == col_h[None, :],
                    row_slope[:, None] * (col_pos[None, :] - row_t[:, None]), NEG_INF).astype(np.float32)
    ncol_t = np.repeat(np.arange(n_tok), N_HEADS)
    ncol_h = np.tile(np.arange(N_HEADS), n_tok)
    dist = row_t[:, None] - ncol_t[None, :]
    bias_new = np.where((row_h[:, None] == ncol_h[None, :]) & (dist >= 0),
                        -row_slope[:, None] * dist, NEG_INF).astype(np.float32)
    slope_col = np.ascontiguousarray(row_slope[:, None])

    tok_spec = pl.BlockSpec((1, rows, V_DIM), lambda b, s, pt: (b, 0, 0))

    def const2(shape):
        return pl.BlockSpec(shape, lambda b, s, pt: (0, 0))

    hbm = pl.BlockSpec(memory_space=pl.ANY)
    page_block = (PAGE_SLOTS, g_pages, PAGE_SIZE * N_HEADS, V_DIM)
    grid_spec = pltpu.PrefetchScalarGridSpec(
        num_scalar_prefetch=1,
        grid=(db, n_pages // g_pages),
        in_specs=[tok_spec, tok_spec, tok_spec, hbm, hbm,
                  const2(bias.shape), const2(bias_new.shape), const2(slope_col.shape),
                  const2(lam_p.shape), const2(subln_g.shape)],
        out_specs=tok_spec,
        scratch_shapes=[pltpu.VMEM(page_block, F32), pltpu.VMEM(page_block, F32),
                        pltpu.SemaphoreType.DMA((PAGE_SLOTS,)),
                        pltpu.VMEM((2 * rows, V_DIM), BF16), pltpu.VMEM((2 * rows, 1), F32),
                        pltpu.VMEM((2 * rows, 1), F32), pltpu.VMEM((2 * rows, V_DIM), F32)],
    )
    return pl.pallas_call(
        functools.partial(_attn_decode_kernel, lam_init=lam_init, past_len=past_len, n_tok=n_tok),
        grid_spec=grid_spec,
        out_shape=jax.ShapeDtypeStruct((db, rows, V_DIM), F32),
        compiler_params=_params(2),
        name="attn_decode",
    )(page_table, q, k_new, v_new, cache_k, cache_v, bias, bias_new, slope_col, lam_p, subln_g)


def _mix_ffn_kernel(h_ref, o_ref, sga_ref, sgm_ref, gu_ref, vn_ref, wsp_ref, bsp_ref, wo_ref,
                    g2_ref, b2_ref, wg_ref, wu_ref, wd_ref, g3_ref, b3_ref, y_ref, s_ref, *, period):
    tm = h_ref.shape[0]
    r = lax.broadcasted_iota(jnp.int32, (CHUNK, CHUNK), 0)
    c = lax.broadcasted_iota(jnp.int32, (CHUNK, CHUNK), 1)
    keep = (r // period == c // period) & (c <= r)
    for grp in range(N_GROUPS):
        w = jnp.where(keep, wsp_ref[grp], 0.0).astype(BF16)
        cols = slice(grp * GROUP_WIDTH, (grp + 1) * GROUP_WIDTH)
        for ch in range(tm // CHUNK):
            rws = slice(ch * CHUNK, (ch + 1) * CHUNK)
            f = jnp.dot(w, vn_ref[rws, cols].astype(BF16), preferred_element_type=F32) + bsp_ref[grp]
            s_ref[rws, cols] = gu_ref[rws, cols] * f
    mix_in = sga_ref[...] * o_ref[...] + sgm_ref[...] * s_ref[...]
    mix = jnp.dot(mix_in.astype(BF16), wo_ref[...], preferred_element_type=F32)
    h2 = _layer_norm(ALPHA * h_ref[...] + mix, g2_ref[...], b2_ref[...])
    y = _swiglu(h2.astype(BF16), wg_ref, wu_ref, wd_ref)
    y_ref[...] = _layer_norm(ALPHA * h2 + 0.5 * y, g3_ref[...], b3_ref[...])


def _mix_ffn(h, o, sga, sgm, gu, vn, wsp, bsp, wo, g2, b2, wg, wu, wd, g3, b3, *, tm, period):
    rows = h.shape[0]
    row_spec = pl.BlockSpec((tm, D_MODEL), lambda i: (i, 0))
    consts = (wsp, bsp, wo, g2, b2, wg, wu, wd, g3, b3)
    return pl.pallas_call(
        functools.partial(_mix_ffn_kernel, period=period),
        grid=(rows // tm,),
        in_specs=[row_spec] * 6 + [_const_spec(a.shape) for a in consts],
        out_specs=row_spec,
        out_shape=jax.ShapeDtypeStruct((rows, D_MODEL), F32),
        scratch_shapes=[pltpu.VMEM((tm, D_MODEL), F32)],
        compiler_params=_params(1),
        name="mix_ffn",
    )(h, o, sga, sgm, gu, vn, *consts)


def _spatial_params(w_spatial, b_spatial, period):
    if period == CHUNK:
        w, b = w_spatial, b_spatial
    else:
        sel = np.equal.outer(np.arange(CHUNK) % period, np.arange(period)).astype(np.float32)
        hi = lax.Precision.HIGHEST
        w = jnp.einsum("ra,gab,cb->grc", sel, w_spatial[:, :period, :period], sel, precision=hi)
        b = jnp.einsum("ra,ga->gr", sel, b_spatial[:, :period], precision=hi)
    return w, b[:, :, None]


def kernel(x_prompt, x_sample, cache_k, cache_v, page_table, ln1_g, ln1_b, ffn1_wg, ffn1_wu, ffn1_wd, w_in, lam_q1, lam_k1, lam_q2, lam_k2, subln_g, gmlp_ln_g, gmlp_ln_b, w_spatial, b_spatial, w_out, ln2_g, ln2_b, ffn2_wg, ffn2_wu, ffn2_wd, ln3_g, ln3_b):
    batch, seq, _ = x_prompt.shape
    db, n_tok, _ = x_sample.shape
    rows_p, rows_s = batch * seq, db * n_tok
    assert rows_s == CHUNK and CHUNK % n_tok == 0 and seq % ATTN_TK == 0
    hp = x_prompt.reshape(rows_p, D_MODEL)
    hs = x_sample.reshape(rows_s, D_MODEL)
    row = lambda a: a.reshape(1, -1)
    tm = ATTN_TQ
    kp_l, vp_l, ks_l, vs_l, gv_l = [], [], [], [], []
    for l in range(DEPTH):
        lam_init = 0.8 - 0.6 * math.exp(-0.3 * l)
        wg1, wu1, wd1 = ffn1_wg[l].astype(BF16), ffn1_wu[l].astype(BF16), ffn1_wd[l].astype(BF16)
        wg2, wu2, wd2 = ffn2_wg[l].astype(BF16), ffn2_wu[l].astype(BF16), ffn2_wd[l].astype(BF16)
        w_in_b, w_out_b = w_in[l].astype(BF16), w_out[l].astype(BF16)
        lam_p = jnp.stack([lam_q1[l], lam_k1[l], lam_q2[l], lam_k2[l]])
        sub_g = row(subln_g[l])
        ffn1 = (wg1, wu1, wd1, row(ln1_g[l]), row(ln1_b[l]))
        tail = (w_out_b, row(ln2_g[l]), row(ln2_b[l]), wg2, wu2, wd2, row(ln3_g[l]), row(ln3_b[l]))
        gln = (row(gmlp_ln_g[l]), row(gmlp_ln_b[l]))

        h1 = _ffn_ln(hp, *ffn1, tm=tm)
        k_p, v_p, qt, kb, vt, gu, vn, sga, sgm = _proj(h1, w_in_b, *gln, tm=tm, batch=batch)
        o = _attn_prompt(qt, kb, vt, lam_p, sub_g, lam_init=lam_init)
        wsp, bsp = _spatial_params(w_spatial[l], b_spatial[l], CHUNK)
        hp = _mix_ffn(h1, o, sga, sgm, gu, vn, wsp, bsp, *tail, tm=tm, period=CHUNK)

        h1 = _ffn_ln(hs, *ffn1, tm=rows_s)
        q_s, k_s, v_s, gu, vn_s, sga, sgm = _proj(h1, w_in_b, *gln, tm=rows_s)
        tok = lambda a: a.reshape(db, n_tok * N_HEADS, V_DIM)
        o = _attn_decode(tok(q_s), tok(k_s), tok(v_s),
                         cache_k[l].reshape(-1, PAGE_SIZE * N_HEADS, V_DIM),
                         cache_v[l].reshape(-1, PAGE_SIZE * N_HEADS, V_DIM),
                         page_table, lam_p, sub_g, lam_init=lam_init)
        wsp, bsp = _spatial_params(w_spatial[l], b_spatial[l], n_tok)
        hs = _mix_ffn(h1, o.reshape(rows_s, D_MODEL), sga, sgm, gu, vn_s, wsp, bsp, *tail, tm=rows_s, period=n_tok)

        kp_l.append(k_p.reshape(batch, seq, N_HEADS, V_DIM))
        vp_l.append(v_p.reshape(batch, seq, N_HEADS, V_DIM))
        ks_l.append(k_s.reshape(db, n_tok, N_HEADS, V_DIM))
        vs_l.append(v_s.reshape(db, n_tok, N_HEADS, V_DIM))
        gv_l.append(vn_s.reshape(db, n_tok, D_MODEL))
    return (hp.reshape(batch, seq, D_MODEL), hs.reshape(db, n_tok, D_MODEL),
            jnp.stack(kp_l), jnp.stack(vp_l), jnp.stack(ks_l), jnp.stack(vs_l), jnp.stack(gv_l))
```
